```python
import jax, jax.numpy as jnp
from jax import lax
import numpy as np

D_MODEL = 2048
BATCH = 16
SEQ = 256
DEPTH = 2
DEC_BATCH = 8
DEC_SEQ = 4096
PAST_LEN = 512

GRID_W = 64
HEAD_DIM = 64
ATTN_DIM = D_MODEL // 2
N_HEADS = ATTN_DIM // HEAD_DIM
N_KV_HEADS = 4
Q_PER_KV = N_HEADS // N_KV_HEADS
KV_DIM = N_KV_HEADS * HEAD_DIM
WINDOW = 128
BLOCK = 128
CONV_DIM = D_MODEL // 4
CONV_K = 31
GMLP_DIM = D_MODEL // 4
GMLP_GROUP_DIM = 128
GMLP_GROUPS = GMLP_DIM // GMLP_GROUP_DIM
CHUNK = 128
MIX_DIM = ATTN_DIM + CONV_DIM + GMLP_DIM
IN_COLS = ATTN_DIM + 2 * KV_DIM + 2 * CONV_DIM + 2 * GMLP_DIM
D_FF = 5632
FFN_K = 3
ROPE_THETA = 10000.0
EPS = 1e-6
NEG = -1e30
SCALE = HEAD_DIM ** -0.5

kernel_name = "hybrid_prefix_diffusion_step"


def rmsnorm(x, g):
    xf = x.astype(jnp.float32)
    y = xf * lax.rsqrt(jnp.mean(xf * xf, axis=-1, keepdims=True) + EPS)
    return (y * g.astype(jnp.float32)).astype(x.dtype)


def layernorm(x, g, b):
    xf = x.astype(jnp.float32)
    mu = jnp.mean(xf, axis=-1, keepdims=True)
    var = jnp.mean(jnp.square(xf - mu), axis=-1, keepdims=True)
    y = (xf - mu) * lax.rsqrt(var + EPS) * g.astype(jnp.float32) + b.astype(jnp.float32)
    return y.astype(x.dtype)


def dwconv(x, w, b):
    pad = (w.shape[0] - 1) // 2
    y = lax.conv_general_dilated(x, w[:, None, :].astype(x.dtype), window_strides=(1,),
                                 padding=[(pad, pad)],
                                 dimension_numbers=('NWC', 'WIO', 'NWC'),
                                 feature_group_count=x.shape[-1])
    return y + b


def axial_rope(L):
    rows = L // GRID_W
    row = jnp.repeat(jnp.arange(rows, dtype=jnp.float32), GRID_W)
    col = jnp.tile(jnp.arange(GRID_W, dtype=jnp.float32), rows)
    n_freq = HEAD_DIM // 4
    inv = ROPE_THETA ** (-jnp.arange(n_freq, dtype=jnp.float32) / n_freq)
    ang = jnp.concatenate([row[:, None] * inv, col[:, None] * inv], axis=-1)
    return jnp.cos(ang), jnp.sin(ang)


def apply_rope(x, cos, sin):
    half = HEAD_DIM // 2
    c = cos[None, :, None, :].astype(x.dtype)
    s = sin[None, :, None, :].astype(x.dtype)
    x1, x2 = x[..., :half], x[..., half:]
    return jnp.concatenate([x1 * c - x2 * s, x1 * s + x2 * c], axis=-1)


def sink_softmax(s, sink):
    sk = sink.astype(jnp.float32).reshape(N_KV_HEADS, Q_PER_KV)[None, :, :, None, None]
    m = jnp.maximum(jnp.max(s, axis=-1, keepdims=True), sk)
    p = jnp.exp(s - m)
    return p / (jnp.sum(p, axis=-1, keepdims=True) + jnp.exp(sk - m))


def context_attention(q, k, v, sink):
    B, L = q.shape[0], q.shape[1]
    nb = L // BLOCK
    qb = q.reshape(B, nb, BLOCK, N_KV_HEADS, Q_PER_KV, HEAD_DIM).transpose(1, 0, 2, 3, 4, 5)

    def one(qi):
        s = jnp.einsum('bqkgd,bskd->bkgqs', qi, k).astype(jnp.float32) * SCALE
        p = sink_softmax(s, sink).astype(v.dtype)
        return jnp.einsum('bkgqs,bskd->bqkgd', p, v)

    o = lax.map(one, qb)
    return o.transpose(1, 0, 2, 3, 4, 5).reshape(B, L, ATTN_DIM)


def latent_attention(q, k, v, ck, cv, sink):
    B, L = q.shape[0], q.shape[1]
    nb = L // BLOCK
    qb = q.reshape(B, nb, BLOCK, N_KV_HEADS, Q_PER_KV, HEAD_DIM)
    pad = ((0, 0), (BLOCK, BLOCK), (0, 0), (0, 0))
    kp = jnp.pad(k, pad).reshape(B, nb + 2, BLOCK, N_KV_HEADS, HEAD_DIM)
    vp = jnp.pad(v, pad).reshape(B, nb + 2, BLOCK, N_KV_HEADS, HEAD_DIM)
    kw = jnp.concatenate([kp[:, :-2], kp[:, 1:-1], kp[:, 2:]], axis=2)
    vw = jnp.concatenate([vp[:, :-2], vp[:, 1:-1], vp[:, 2:]], axis=2)
    qpos = jnp.arange(BLOCK)
    kpos = jnp.arange(3 * BLOCK) - BLOCK
    band = jnp.abs(kpos[None, :] - qpos[:, None]) <= WINDOW

    def one(args):
        qi, ki, vi, n = args
        kabs = n * BLOCK + kpos
        valid = band & ((kabs >= 0) & (kabs < L))[None, :]
        s_loc = jnp.einsum('bqkgd,bskd->bkgqs', qi, ki).astype(jnp.float32) * SCALE
        s_loc = jnp.where(valid, s_loc, NEG)
        s_ctx = jnp.einsum('bqkgd,bskd->bkgqs', qi, ck).astype(jnp.float32) * SCALE
        p = sink_softmax(jnp.concatenate([s_loc, s_ctx], axis=-1), sink).astype(vi.dtype)
        return (jnp.einsum('bkgqs,bskd->bqkgd', p[..., :3 * BLOCK], vi)
                + jnp.einsum('bkgqs,bskd->bqkgd', p[..., 3 * BLOCK:], cv))

    xs = (qb.transpose(1, 0, 2, 3, 4, 5), kw.transpose(1, 0, 2, 3, 4),
          vw.transpose(1, 0, 2, 3, 4), jnp.arange(nb, dtype=jnp.int32))
    o = lax.map(one, xs)
    return o.transpose(1, 0, 2, 3, 4, 5).reshape(B, L, ATTN_DIM)


def conformer_conv(z, dw_w, dw_b, ln_g, ln_b):
    a, g = jnp.split(z, 2, axis=-1)
    h = dwconv(a * jax.nn.sigmoid(g), dw_w, dw_b)
    return jax.nn.silu(layernorm(h, ln_g, ln_b))


def chunk_gmlp(z, ln_g, ln_b, ws, bs):
    B, L = z.shape[0], z.shape[1]
    nc = L // CHUNK
    u, v = jnp.split(jax.nn.gelu(z), 2, axis=-1)
    v = layernorm(v.reshape(B, L, GMLP_GROUPS, GMLP_GROUP_DIM), ln_g, ln_b)
    v = v.reshape(B, nc, CHUNK, GMLP_GROUPS, GMLP_GROUP_DIM)
    sv = jnp.einsum('gpq,bnqgc->bnpgc', ws, v) + bs.T[:, :, None]
    return u * sv.reshape(B, L, GMLP_DIM)


def token_mixers(h, p, rope, ctx_k, ctx_v):
    B, L = h.shape[0], h.shape[1]
    z = h @ p['w_in']
    o1 = ATTN_DIM
    o2 = o1 + KV_DIM
    o3 = o2 + KV_DIM
    o4 = o3 + 2 * CONV_DIM
    q = z[..., :o1].reshape(B, L, N_HEADS, HEAD_DIM)
    k = z[..., o1:o2].reshape(B, L, N_KV_HEADS, HEAD_DIM)
    v = z[..., o2:o3].reshape(B, L, N_KV_HEADS, HEAD_DIM)
    if rope is None:
        attn = context_attention(q, k, v, p['attn_sink'])
    else:
        cos, sin = rope
        attn = latent_attention(apply_rope(q, cos, sin), apply_rope(k, cos, sin), v,
                                ctx_k, ctx_v, p['attn_sink'])
    conv = conformer_conv(z[..., o3:o4], p['conv_dw_w'], p['conv_dw_b'],
                          p['conv_ln_g'], p['conv_ln_b'])
    gm = chunk_gmlp(z[..., o4:], p['gmlp_ln_g'], p['gmlp_ln_b'], p['gmlp_ws'], p['gmlp_bs'])
    out = jnp.concatenate([attn, conv, gm], axis=-1) @ p['w_out']
    return out, k, v


def conv_ffn(h, w_up, dw_w, dw_b, w_down):
    a, b = jnp.split(dwconv(h @ w_up, dw_w, dw_b), 2, axis=-1)
    return (jax.nn.silu(a) * b) @ w_down


def trunk_layer(x, cond, p, rope, ctx_k, ctx_v):
    mod = (jax.nn.silu(cond) @ p['w_ada'] + p['b_ada'])[:, None, :]
    sh1, sc1, g1, sh2, sc2, g2 = jnp.split(mod, 6, axis=-1)
    h = rmsnorm(x, p['g_norm1']) * (1 + sc1) + sh1
    mix, k, v = token_mixers(h, p, rope, ctx_k, ctx_v)
    x = x + g1 * mix
    h = rmsnorm(x, p['g_norm2']) * (1 + sc2) + sh2
    x = x + g2 * conv_ffn(h, p['w_up'], p['ffn_dw_w'], p['ffn_dw_b'], p['w_down'])
    return x, k, v


def setup_inputs(seed: int = 0) -> dict:
    key = jax.random.key(seed)
    ks = jax.random.split(key, 32)
    nrm = lambda k, shape, s: jax.random.normal(k, shape, jnp.float32) * s
    return {
        'x_prompt': nrm(ks[0], (BATCH, SEQ, D_MODEL), 1.0),
        'x_sample': nrm(ks[1], (DEC_BATCH, DEC_SEQ, D_MODEL), 1.0),
        'cache_k': nrm(ks[2], (DEC_BATCH, DEPTH, PAST_LEN, N_KV_HEADS, HEAD_DIM), 1.0),
        'cache_v': nrm(ks[3], (DEC_BATCH, DEPTH, PAST_LEN, N_KV_HEADS, HEAD_DIM), 1.0),
        'c': nrm(ks[4], (DEC_BATCH, D_MODEL), 1.0),
        'c_ctx': nrm(ks[5], (D_MODEL,), 1.0),
        'w_ada': nrm(ks[6], (DEPTH, D_MODEL, 6 * D_MODEL), 0.5 * D_MODEL ** -0.5),
        'b_ada': nrm(ks[7], (DEPTH, 6 * D_MODEL), 0.02),
        'g_norm1': 1.0 + nrm(ks[8], (DEPTH, D_MODEL), 0.02),
        'w_in': nrm(ks[9], (DEPTH, D_MODEL, IN_COLS), D_MODEL ** -0.5),
        'attn_sink': nrm(ks[10], (DEPTH, N_HEADS), 1.0),
        'conv_dw_w': nrm(ks[11], (DEPTH, CONV_K, CONV_DIM), CONV_K ** -0.5),
        'conv_dw_b': nrm(ks[12], (DEPTH, CONV_DIM), 0.02),
        'conv_ln_g': 1.0 + nrm(ks[13], (DEPTH, CONV_DIM), 0.02),
        'conv_ln_b': nrm(ks[14], (DEPTH, CONV_DIM), 0.02),
        'gmlp_ln_g': 1.0 + nrm(ks[15], (DEPTH, GMLP_GROUPS, GMLP_GROUP_DIM), 0.02),
        'gmlp_ln_b': nrm(ks[16], (DEPTH, GMLP_GROUPS, GMLP_GROUP_DIM), 0.02),
        'gmlp_ws': nrm(ks[17], (DEPTH, GMLP_GROUPS, CHUNK, CHUNK), CHUNK ** -0.5),
        'gmlp_bs': 1.0 + nrm(ks[18], (DEPTH, GMLP_GROUPS, CHUNK), 0.02),
        'w_out': nrm(ks[19], (DEPTH, MIX_DIM, D_MODEL), MIX_DIM ** -0.5),
        'g_norm2': 1.0 + nrm(ks[20], (DEPTH, D_MODEL), 0.02),
        'w_up': nrm(ks[21], (DEPTH, D_MODEL, 2 * D_FF), D_MODEL ** -0.5),
        'ffn_dw_w': nrm(ks[22], (DEPTH, FFN_K, 2 * D_FF), FFN_K ** -0.5),
        'ffn_dw_b': nrm(ks[23], (DEPTH, 2 * D_FF), 0.02),
        'w_down': nrm(ks[24], (DEPTH, D_FF, D_MODEL), D_FF ** -0.5),
        'g_final': 1.0 + nrm(ks[25], (D_MODEL,), 0.02),
    }


def reference(x_prompt, x_sample, cache_k, cache_v, c, c_ctx, w_ada, b_ada, g_norm1, w_in,
              attn_sink, conv_dw_w, conv_dw_b, conv_ln_g, conv_ln_b, gmlp_ln_g, gmlp_ln_b,
              gmlp_ws, gmlp_bs, w_out, g_norm2, w_up, ffn_dw_w, ffn_dw_b, w_down, g_final):
    rope = axial_rope(x_sample.shape[1])
    cond_ctx = c_ctx[None, :]
    xp = x_prompt
    xs = x_sample
    new_ks = []
    new_vs = []
    for l in range(DEPTH):
        p = {
            'w_ada': w_ada[l], 'b_ada': b_ada[l], 'g_norm1': g_norm1[l], 'w_in': w_in[l],
            'attn_sink': attn_sink[l], 'conv_dw_w': conv_dw_w[l], 'conv_dw_b': conv_dw_b[l],
            'conv_ln_g': conv_ln_g[l], 'conv_ln_b': conv_ln_b[l], 'gmlp_ln_g': gmlp_ln_g[l],
            'gmlp_ln_b': gmlp_ln_b[l], 'gmlp_ws': gmlp_ws[l], 'gmlp_bs': gmlp_bs[l],
            'w_out': w_out[l], 'g_norm2': g_norm2[l], 'w_up': w_up[l],
            'ffn_dw_w': ffn_dw_w[l], 'ffn_dw_b': ffn_dw_b[l], 'w_down': w_down[l],
        }
        xp, kc, vc = trunk_layer(xp, cond_ctx, p, None, None, None)
        new_ks.append(kc)
        new_vs.append(vc)
        xs, _, _ = trunk_layer(xs, c, p, rope, cache_k[:, l], cache_v[:, l])
    y_prompt = rmsnorm(xp, g_final)
    y_sample = rmsnorm(xs, g_final)
    new_k = jnp.stack(new_ks, axis=1)
    new_v = jnp.stack(new_vs, axis=1)
    return (y_prompt, y_sample, new_k, new_v)
```

```python
import functools

import jax
import jax.numpy as jnp
from jax import lax
from jax.experimental import pallas as pl
from jax.experimental.pallas import tpu as pltpu

F32 = jnp.float32
BF16 = jnp.bfloat16

HEAD_DIM = 64
N_KV_HEADS = 4
Q_PER_KV = 4
GRID_W = 64
BLOCK = 128
CHUNK = 128
GMLP_GROUP_DIM = 128
CONV_K = 31
FFN_K = 3
ROPE_THETA = 10000.0
EPS = 1e-6
NEG = -1e30
SCALE = HEAD_DIM ** -0.5

V7X_LANES = 128
V7X_SUBLANES = 8
V7X_VMEM_LIMIT_BYTES = 56 * 1024 * 1024

MIX_TILE_ROWS = 512
FFN_TILE_ROWS = 1024
FFN_TILE_COLS = 256

CONV_HALO = 16
FFN_HALO = V7X_SUBLANES


def _sigmoid(x):
    return 1.0 / (1.0 + jnp.exp(-x))


def _rms_mod(x, g, sc, sh):
    ms = jnp.mean(x * x, axis=-1, keepdims=True)
    return (x * lax.rsqrt(ms + EPS) * g) * (1.0 + sc) + sh


def _const_spec(shape):
    nd = len(shape)
    return pl.BlockSpec(shape, lambda *_: (0,) * nd, pipeline_mode=pl.Buffered(1))


def _params(sem):
    return pltpu.CompilerParams(dimension_semantics=sem, vmem_limit_bytes=V7X_VMEM_LIMIT_BYTES)


def _ada_kernel(c_ref, w_ref, b_ref, o_ref):
    c = c_ref[...]
    s = (c * _sigmoid(c)).astype(BF16)
    o_ref[0] = jnp.dot(s, w_ref[0].astype(BF16), preferred_element_type=F32) + b_ref[0]


def _ada(cond, w_ada, b_ada):
    depth, d, n = w_ada.shape
    r = cond.shape[0]
    tn = 1024
    return pl.pallas_call(
        _ada_kernel,
        grid=(depth, n // tn),
        in_specs=[
            pl.BlockSpec((r, d), lambda l, j: (0, 0)),
            pl.BlockSpec((1, d, tn), lambda l, j: (l, 0, j)),
            pl.BlockSpec((1, 1, tn), lambda l, j: (l, 0, j)),
        ],
        out_specs=pl.BlockSpec((1, r, tn), lambda l, j: (l, 0, j)),
        out_shape=jax.ShapeDtypeStruct((depth, r, n), F32),
        compiler_params=_params(("arbitrary", "arbitrary")),
        name="ada",
    )(cond, w_ada, b_ada.reshape(depth, 1, n))


def _rope(t, cos, sin):
    outs = []
    lane = lax.broadcasted_iota(jnp.int32, (t.shape[0], V7X_LANES), 1)
    first = (lane % HEAD_DIM) < (HEAD_DIM // 2)
    for j in range(t.shape[1] // V7X_LANES):
        tj = t[:, j * V7X_LANES:(j + 1) * V7X_LANES]
        rot = jnp.where(first, pltpu.roll(tj, V7X_LANES - HEAD_DIM // 2, 1), pltpu.roll(tj, HEAD_DIM // 2, 1))
        outs.append(tj * cos + rot * sin)
    return jnp.concatenate(outs, axis=1)


def _inproj_kernel(*refs, tm, dims, use_rope, emit_kv_f32):
    attn_dim, kv_dim, conv_dim, gmlp_dim = dims
    it = iter(refs)
    x_ref, mod_ref, g_ref, w_ref = next(it), next(it), next(it), next(it)
    if use_rope:
        cos_ref, sin_ref = next(it), next(it)
    lng_ref, lnb_ref, ws_ref, bs_ref = next(it), next(it), next(it), next(it)
    q_ref, k_ref, v_ref, glu_ref, gm_ref = next(it), next(it), next(it), next(it), next(it)
    if emit_kv_f32:
        kf_ref, vf_ref = next(it), next(it)

    h = _rms_mod(x_ref[...], g_ref[...], mod_ref[0, 1:2, :], mod_ref[0, 0:1, :]).astype(BF16)

    def proj(lo, n):
        return jnp.dot(h, w_ref[:, lo:lo + n], preferred_element_type=F32)

    o_k = attn_dim
    o_v = o_k + kv_dim
    o_a = o_v + kv_dim
    o_g = o_a + conv_dim
    o_u = o_g + conv_dim
    o_w = o_u + gmlp_dim

    q = proj(0, attn_dim)
    k = proj(o_k, kv_dim)
    v = proj(o_v, kv_dim)
    if emit_kv_f32:
        kf_ref[...] = k
        vf_ref[...] = v
    if use_rope:
        cos, sin = cos_ref[...], sin_ref[...]
        q = _rope(q, cos, sin)
        k = _rope(k, cos, sin)
    q_ref[...] = q.astype(BF16)
    k_ref[...] = k.astype(BF16)
    v_ref[...] = v.astype(BF16)

    glu_ref[...] = proj(o_a, conv_dim) * _sigmoid(proj(o_g, conv_dim))

    u = jax.nn.gelu(proj(o_u, gmlp_dim))
    w = jax.nn.gelu(proj(o_w, gmlp_dim))
    for g in range(gmlp_dim // GMLP_GROUP_DIM):
        cs = slice(g * GMLP_GROUP_DIM, (g + 1) * GMLP_GROUP_DIM)
        wg = w[:, cs]
        mu = jnp.mean(wg, axis=-1, keepdims=True)
        var = jnp.mean(jnp.square(wg - mu), axis=-1, keepdims=True)
        wn = ((wg - mu) * lax.rsqrt(var + EPS) * lng_ref[:, cs] + lnb_ref[:, cs]).astype(BF16)
        for c in range(tm // CHUNK):
            rs = slice(c * CHUNK, (c + 1) * CHUNK)
            sv = jnp.dot(ws_ref[g], wn[rs, :], preferred_element_type=F32) + bs_ref[:, cs]
            gm_ref[rs, cs] = (u[rs, cs] * sv).astype(BF16)


def _inproj(x, mod, g_norm, w_in, rope, gmlp, *, tm, tiles_per_mod, dims, emit_kv_f32):
    n, d = x.shape
    attn_dim, kv_dim, conv_dim, gmlp_dim = dims
    lng, lnb, ws, bs = gmlp
    row = lambda w: pl.BlockSpec((tm, w), lambda i: (i, 0))
    in_specs = [
        row(d),
        pl.BlockSpec((1, 6, d), lambda i: (i // tiles_per_mod, 0, 0)),
        _const_spec((1, d)),
        _const_spec(w_in.shape),
    ]
    args = [x, mod, g_norm, w_in]
    if rope is not None:
        tiles_per_seq = rope[0].shape[0] // tm
        in_specs += [pl.BlockSpec((tm, V7X_LANES), lambda i: (i % tiles_per_seq, 0))] * 2
        args += list(rope)
    in_specs += [_const_spec(lng.shape), _const_spec(lnb.shape), _const_spec(ws.shape), _const_spec(bs.shape)]
    args += [lng, lnb, ws, bs]
    out_specs = [row(attn_dim), row(kv_dim), row(kv_dim), row(conv_dim), row(gmlp_dim)]
    out_shape = [
        jax.ShapeDtypeStruct((n, attn_dim), BF16),
        jax.ShapeDtypeStruct((n, kv_dim), BF16),
        jax.ShapeDtypeStruct((n, kv_dim), BF16),
        jax.ShapeDtypeStruct((n, conv_dim), F32),
        jax.ShapeDtypeStruct((n, gmlp_dim), BF16),
    ]
    if emit_kv_f32:
        out_specs += [row(kv_dim), row(kv_dim)]
        out_shape += [jax.ShapeDtypeStruct((n, kv_dim), F32)] * 2
    return pl.pallas_call(
        functools.partial(_inproj_kernel, tm=tm, dims=dims, use_rope=rope is not None, emit_kv_f32=emit_kv_f32),
        grid=(n // tm,),
        in_specs=in_specs,
        out_specs=out_specs,
        out_shape=out_shape,
        compiler_params=_params(("arbitrary",)),
        name="inproj",
    )(*args)


def _attn_kernel(*refs, nb, local):
    it = iter(refs)
    sink_ref, q_ref = next(it), next(it)
    if local:
        kl_refs = (next(it), next(it), next(it))
        vl_refs = (next(it), next(it), next(it))
    ck_ref, cv_ref, o_ref = next(it), next(it), next(it)

    rows = Q_PER_KV * BLOCK
    lane = lax.broadcasted_iota(jnp.int32, (rows, V7X_LANES), 1)
    rowg = lax.broadcasted_iota(jnp.int32, (rows, 1), 0) // BLOCK
    upper = lane >= HEAD_DIM
    nt = (((1,), (1,)), ((), ()))

    if local:
        n = pl.program_id(0) % nb
        qi = lax.broadcasted_iota(jnp.int32, (rows, 3 * BLOCK), 0) % BLOCK
        kj = lax.broadcasted_iota(jnp.int32, (rows, 3 * BLOCK), 1)
        lo = jnp.where(n == 0, BLOCK, 0)
        hi = jnp.where(n == nb - 1, 2 * BLOCK, 3 * BLOCK)
        valid = (kj >= qi) & (kj <= qi + 2 * BLOCK) & (kj >= lo) & (kj < hi)

    for p in range(N_KV_HEADS // 2):
        ls = slice(p * V7X_LANES, (p + 1) * V7X_LANES)
        qs = jnp.concatenate(
            [q_ref[:, g * 2 * V7X_LANES + p * V7X_LANES: g * 2 * V7X_LANES + (p + 1) * V7X_LANES]
             for g in range(Q_PER_KV)], axis=0) * SCALE
        ck = ck_ref[0, :, ls]
        cv = cv_ref[0, :, ls]
        if local:
            kl = jnp.concatenate([r[:, ls] for r in kl_refs], axis=0)
            vl = jnp.concatenate([r[:, ls] for r in vl_refs], axis=0)
        outs = []
        for hh in range(2):
            kvh = 2 * p + hh
            qm = jnp.where(upper if hh else jnp.logical_not(upper), qs, jnp.zeros_like(qs))
            sk = jnp.zeros((rows, 1), F32)
            for g in range(Q_PER_KV):
                sk = jnp.where(rowg == g, sink_ref[kvh * Q_PER_KV + g], sk)
            s_ctx = lax.dot_general(qm, ck, nt, preferred_element_type=F32)
            m = jnp.maximum(jnp.max(s_ctx, axis=-1, keepdims=True), sk)
            if local:
                s_loc = lax.dot_general(qm, kl, nt, preferred_element_type=F32)
                s_loc = jnp.where(valid, s_loc, NEG)
                m = jnp.maximum(m, jnp.max(s_loc, axis=-1, keepdims=True))
            p_ctx = jnp.exp(s_ctx - m)
            den = jnp.sum(p_ctx, axis=-1, keepdims=True) + jnp.exp(sk - m)
            acc = jnp.dot(p_ctx.astype(BF16), cv, preferred_element_type=F32)
            if local:
                p_loc = jnp.exp(s_loc - m)
                den = den + jnp.sum(p_loc, axis=-1, keepdims=True)
                acc = acc + jnp.dot(p_loc.astype(BF16), vl, preferred_element_type=F32)
            outs.append(acc / den)
        o = jnp.where(upper, outs[1], outs[0]).astype(BF16)
        for g in range(Q_PER_KV):
            o_ref[:, g * 2 * V7X_LANES + p * V7X_LANES: g * 2 * V7X_LANES + (p + 1) * V7X_LANES] = (
                o[g * BLOCK:(g + 1) * BLOCK, :])


def _attention(q, k, v, ck, cv, sink, *, nb, local):
    n, attn_dim = q.shape
    kv_dim = ck.shape[-1]
    nblk = n // BLOCK
    in_specs = [pl.BlockSpec(memory_space=pltpu.SMEM), pl.BlockSpec((BLOCK, attn_dim), lambda i: (i, 0))]
    args = [sink, q]
    if local:
        maps = [lambda i: (jnp.maximum(i - 1, 0), 0), lambda i: (i, 0), lambda i: (jnp.minimum(i + 1, nblk - 1), 0)]
        in_specs += [pl.BlockSpec((BLOCK, kv_dim), m) for m in maps] * 2
        args += [k, k, k, v, v, v]
    in_specs += [pl.BlockSpec((1,) + ck.shape[1:], lambda i: (i // nb, 0, 0))] * 2
    args += [ck, cv]
    return pl.pallas_call(
        functools.partial(_attn_kernel, nb=nb, local=local),
        grid=(nblk,),
        in_specs=in_specs,
        out_specs=pl.BlockSpec((BLOCK, attn_dim), lambda i: (i, 0)),
        out_shape=jax.ShapeDtypeStruct((n, attn_dim), BF16),
        compiler_params=_params(("arbitrary",)),
        name="attention",
    )(*args)


def _outproj_kernel(attn_ref, glu_ref, glup_ref, glun_ref, gm_ref, x_ref, mod_ref, cw_ref, cb_ref,
                    clg_ref, clb_ref, wo_ref, o_ref, pad_ref, *, tm, tiles_per_seq):
    n = pl.program_id(0) % tiles_per_seq
    attn_dim = attn_ref.shape[1]
    conv_dim = glu_ref.shape[1]
    pad_ref[0:CONV_HALO, :] = jnp.where(n > 0, glup_ref[...], 0.0)
    pad_ref[CONV_HALO:CONV_HALO + tm, :] = glu_ref[...]
    pad_ref[CONV_HALO + tm:, :] = jnp.where(n < tiles_per_seq - 1, glun_ref[...], 0.0)

    base = CONV_HALO - (CONV_K - 1) // 2
    acc = pad_ref[base:base + tm, :] * cw_ref[0:1, :]
    for j in range(1, CONV_K):
        acc = acc + pad_ref[base + j:base + j + tm, :] * cw_ref[j:j + 1, :]
    acc = acc + cb_ref[...]
    mu = jnp.mean(acc, axis=-1, keepdims=True)
    var = jnp.mean(jnp.square(acc - mu), axis=-1, keepdims=True)
    y = (acc - mu) * lax.rsqrt(var + EPS) * clg_ref[...] + clb_ref[...]
    conv = (y * _sigmoid(y)).astype(BF16)

    mix = jnp.dot(attn_ref[...], wo_ref[0:attn_dim, :], preferred_element_type=F32)
    mix = mix + jnp.dot(conv, wo_ref[attn_dim:attn_dim + conv_dim, :], preferred_element_type=F32)
    mix = mix + jnp.dot(gm_ref[...], wo_ref[attn_dim + conv_dim:, :], preferred_element_type=F32)
    o_ref[...] = x_ref[...] + mod_ref[0, 2:3, :] * mix


def _outproj(attn, glu, gm, x, mod, conv, w_out, *, tm, tiles_per_seq, tiles_per_mod):
    n, d = x.shape
    cw, cb, clg, clb = conv
    hb = tm // CONV_HALO
    nhb = n // CONV_HALO
    row = lambda w: pl.BlockSpec((tm, w), lambda i: (i, 0))
    return pl.pallas_call(
        functools.partial(_outproj_kernel, tm=tm, tiles_per_seq=tiles_per_seq),
        grid=(n // tm,),
        in_specs=[
            row(attn.shape[1]),
            row(glu.shape[1]),
            pl.BlockSpec((CONV_HALO, glu.shape[1]), lambda i: (jnp.maximum(i * hb - 1, 0), 0)),
            pl.BlockSpec((CONV_HALO, glu.shape[1]), lambda i: (jnp.minimum((i + 1) * hb, nhb - 1), 0)),
            row(gm.shape[1]),
            row(d),
            pl.BlockSpec((1, 6, d), lambda i: (i // tiles_per_mod, 0, 0)),
            _const_spec(cw.shape), _const_spec(cb.shape), _const_spec(clg.shape), _const_spec(clb.shape),
            _const_spec(w_out.shape),
        ],
        out_specs=row(d),
        out_shape=jax.ShapeDtypeStruct((n, d), F32),
        scratch_shapes=[pltpu.VMEM((tm + 2 * CONV_HALO, glu.shape[1]), F32)],
        compiler_params=_params(("arbitrary",)),
        name="outproj",
    )(attn, glu, glu, glu, gm, x, mod, cw, cb, clg, clb, w_out)


def _ffn_kernel(*refs, tm, tiles_per_seq, nf, final):
    it = iter(refs)
    x_ref, xp_ref, xn_ref, mod_ref, g_ref = next(it), next(it), next(it), next(it), next(it)
    wa_ref, wb_ref, cwa_ref, cwb_ref, cba_ref, cbb_ref, wd_ref = (next(it) for _ in range(7))
    if final:
        gf_ref = next(it)
    o_ref, h_ref, hh_ref = next(it), next(it), next(it)

    f = pl.program_id(1)
    n = pl.program_id(0) % tiles_per_seq

    @pl.when(f == 0)
    def _():
        g, sc, sh = g_ref[...], mod_ref[0, 4:5, :], mod_ref[0, 3:4, :]
        x = x_ref[...]
        h_ref[...] = _rms_mod(x, g, sc, sh).astype(BF16)
        hp = jnp.where(n > 0, _rms_mod(xp_ref[...], g, sc, sh), 0.0)
        hn = jnp.where(n < tiles_per_seq - 1, _rms_mod(xn_ref[...], g, sc, sh), 0.0)
        hh_ref[0:FFN_HALO, :] = hp.astype(BF16)
        hh_ref[FFN_HALO:, :] = hn.astype(BF16)
        o_ref[...] = x

    h = h_ref[...]
    hh = hh_ref[...]
    row = lax.broadcasted_iota(jnp.int32, (tm, 1), 0)

    def branch(w_ref, cw_ref, cb_ref):
        u = jnp.dot(h, w_ref[...], preferred_element_type=F32)
        uh = jnp.dot(hh, w_ref[...], preferred_element_type=F32)
        up = jnp.where(row == 0, uh[FFN_HALO - 1:FFN_HALO, :], pltpu.roll(u, 1, 0))
        dn = jnp.where(row == tm - 1, uh[FFN_HALO:FFN_HALO + 1, :], pltpu.roll(u, tm - 1, 0))
        return cw_ref[0:1, :] * up + cw_ref[1:2, :] * u + cw_ref[2:3, :] * dn + cb_ref[...]

    a = branch(wa_ref, cwa_ref, cba_ref)
    b = branch(wb_ref, cwb_ref, cbb_ref)
    act = (a * _sigmoid(a) * b).astype(BF16)
    o_ref[...] += mod_ref[0, 5:6, :] * jnp.dot(act, wd_ref[...], preferred_element_type=F32)

    if final:
        @pl.when(f == nf - 1)
        def _():
            y = o_ref[...]
            ms = jnp.mean(y * y, axis=-1, keepdims=True)
            o_ref[...] = y * lax.rsqrt(ms + EPS) * gf_ref[...]


def _ffn(x, mod, g_norm, w_up, cw, cb, w_down, g_final, *, tm, tf, tiles_per_seq, tiles_per_mod):
    n, d = x.shape
    d_ff = w_down.shape[0]
    nf = d_ff // tf
    hb = tm // FFN_HALO
    nhb = n // FFN_HALO
    final = g_final is not None
    in_specs = [
        pl.BlockSpec((tm, d), lambda i, f: (i, 0)),
        pl.BlockSpec((FFN_HALO, d), lambda i, f: (jnp.maximum(i * hb - 1, 0), 0)),
        pl.BlockSpec((FFN_HALO, d), lambda i, f: (jnp.minimum((i + 1) * hb, nhb - 1), 0)),
        pl.BlockSpec((1, 6, d), lambda i, f: (i // tiles_per_mod, 0, 0)),
        _const_spec((1, d)),
        pl.BlockSpec((d, tf), lambda i, f: (0, f)),
        pl.BlockSpec((d, tf), lambda i, f: (0, nf + f)),
        pl.BlockSpec((FFN_K, tf), lambda i, f: (0, f)),
        pl.BlockSpec((FFN_K, tf), lambda i, f: (0, nf + f)),
        pl.BlockSpec((1, tf), lambda i, f: (0, f)),
        pl.BlockSpec((1, tf), lambda i, f: (0, nf + f)),
        pl.BlockSpec((tf, d), lambda i, f: (f, 0)),
    ]
    args = [x, x, x, mod, g_norm, w_up, w_up, cw, cw, cb, cb, w_down]
    if final:
        in_specs.append(_const_spec((1, d)))
        args.append(g_final)
    return pl.pallas_call(
        functools.partial(_ffn_kernel, tm=tm, tiles_per_seq=tiles_per_seq, nf=nf, final=final),
        grid=(n // tm, nf),
        in_specs=in_specs,
        out_specs=pl.BlockSpec((tm, d), lambda i, f: (i, 0)),
        out_shape=jax.ShapeDtypeStruct((n, d), F32),
        scratch_shapes=[pltpu.VMEM((tm, d), BF16), pltpu.VMEM((2 * FFN_HALO, d), BF16)],
        compiler_params=_params(("arbitrary", "arbitrary")),
        name="ffn",
    )(*args)


def _rope_tables(seq_len):
    rows = seq_len // GRID_W
    row = jnp.repeat(jnp.arange(rows, dtype=F32), GRID_W)
    col = jnp.tile(jnp.arange(GRID_W, dtype=F32), rows)
    n_freq = HEAD_DIM // 4
    inv = ROPE_THETA ** (-jnp.arange(n_freq, dtype=F32) / n_freq)
    ang = jnp.concatenate([row[:, None] * inv, col[:, None] * inv], axis=-1)
    cos, sin = jnp.cos(ang), jnp.sin(ang)
    reps = V7X_LANES // HEAD_DIM
    return jnp.tile(jnp.concatenate([cos, cos], -1), (1, reps)), jnp.tile(jnp.concatenate([-sin, sin], -1), (1, reps))


def _gmajor_cols(w, d_in):
    return w.reshape(d_in, N_KV_HEADS, Q_PER_KV, HEAD_DIM).transpose(0, 2, 1, 3).reshape(d_in, -1)


def _tile_rows(seq_len, target):
    return min(seq_len, target)


def kernel(x_prompt, x_sample, cache_k, cache_v, c, c_ctx, w_ada, b_ada, g_norm1, w_in, attn_sink, conv_dw_w, conv_dw_b, conv_ln_g, conv_ln_b, gmlp_ln_g, gmlp_ln_b, gmlp_ws, gmlp_bs, w_out, g_norm2, w_up, ffn_dw_w, ffn_dw_b, w_down, g_final):
    b1, l1, d = x_prompt.shape
    b2, l2, _ = x_sample.shape
    depth = w_ada.shape[0]
    past = cache_k.shape[2]
    kv_dim = N_KV_HEADS * HEAD_DIM
    attn_dim = N_KV_HEADS * Q_PER_KV * HEAD_DIM
    conv_dim = conv_dw_w.shape[-1]
    gmlp_dim = gmlp_ln_g.shape[1] * gmlp_ln_g.shape[2]
    dims = (attn_dim, kv_dim, conv_dim, gmlp_dim)

    n_cond = 1 + b2
    cond = jnp.concatenate([c_ctx[None, :], c], axis=0)
    cond = jnp.pad(cond, ((0, (-n_cond) % V7X_SUBLANES), (0, 0)))
    mod_all = _ada(cond, w_ada, b_ada)

    rope = _rope_tables(l2)
    xp = x_prompt.reshape(b1 * l1, d)
    xs = x_sample.reshape(b2 * l2, d)
    new_k, new_v = [], []

    for l in range(depth):
        w_in_l = jnp.concatenate([_gmajor_cols(w_in[l, :, :attn_dim], d), w_in[l, :, attn_dim:]], axis=1).astype(BF16)
        w_out_l = jnp.concatenate([_gmajor_cols(w_out[l, :attn_dim].T, d).T, w_out[l, attn_dim:]], axis=0).astype(BF16)
        w_up_l = w_up[l].astype(BF16)
        w_down_l = w_down[l].astype(BF16)
        gmlp = (gmlp_ln_g[l].reshape(1, gmlp_dim), gmlp_ln_b[l].reshape(1, gmlp_dim), gmlp_ws[l].astype(BF16),
                jnp.repeat(gmlp_bs[l].T, GMLP_GROUP_DIM, axis=1))
        conv = (conv_dw_w[l], conv_dw_b[l][None, :], conv_ln_g[l][None, :], conv_ln_b[l][None, :])
        mod = mod_all[l].reshape(-1, 6, d)
        g_last = g_final[None, :] if l == depth - 1 else None

        def layer(x, seq_len, mod_rows, per_seq_mod, rope_l, window_kv, ctx_kv, emit_kv):
            tm = _tile_rows(seq_len, MIX_TILE_ROWS)
            tps = seq_len // tm
            tpm = tps if per_seq_mod else x.shape[0] // tm
            outs = _inproj(x, mod_rows, g_norm1[l][None, :], w_in_l, rope_l, gmlp, tm=tm, tiles_per_mod=tpm,
                           dims=dims, emit_kv_f32=emit_kv)
            q, k, v, glu, gm = outs[:5]
            if window_kv:
                attn = _attention(q, k, v, ctx_kv[0], ctx_kv[1], attn_sink[l], nb=seq_len // BLOCK, local=True)
            else:
                nseq = x.shape[0] // seq_len
                attn = _attention(q, None, None, k.reshape(nseq, seq_len, kv_dim), v.reshape(nseq, seq_len, kv_dim),
                                  attn_sink[l], nb=seq_len // BLOCK, local=False)
            x = _outproj(attn, glu, gm, x, mod_rows, conv, w_out_l, tm=tm, tiles_per_seq=tps, tiles_per_mod=tpm)
            tmf = _tile_rows(seq_len, FFN_TILE_ROWS)
            tpsf = seq_len // tmf
            tpmf = tpsf if per_seq_mod else x.shape[0] // tmf
            x = _ffn(x, mod_rows, g_norm2[l][None, :], w_up_l, ffn_dw_w[l], ffn_dw_b[l][None, :], w_down_l, g_last,
                     tm=tmf, tf=FFN_TILE_COLS, tiles_per_seq=tpsf, tiles_per_mod=tpmf)
            return x, outs[5:]

        xp, kv_f32 = layer(xp, l1, mod[0:1], False, None, False, None, True)
        new_k.append(kv_f32[0].reshape(b1, l1, N_KV_HEADS, HEAD_DIM))
        new_v.append(kv_f32[1].reshape(b1, l1, N_KV_HEADS, HEAD_DIM))
        ctx_kv = (cache_k[:, l].reshape(b2, past, kv_dim).astype(BF16), cache_v[:, l].reshape(b2, past, kv_dim).astype(BF16))
        xs, _ = layer(xs, l2, mod[1:1 + b2], True, rope, True, ctx_kv, False)

    return (xp.reshape(b1, l1, d), xs.reshape(b2, l2, d), jnp.stack(new_k, axis=1), jnp.stack(new_v, axis=1))
```

```python
import functools

import jax
import jax.numpy as jnp
from jax import lax
from jax.experimental import pallas as pl
from jax.experimental.pallas import tpu as pltpu

F32 = jnp.float32
BF16 = jnp.bfloat16

HEAD_DIM = 64
N_KV_HEADS = 4
Q_PER_KV = 4
GRID_W = 64
BLOCK = 128
CHUNK = 128
GMLP_GROUP_DIM = 128
CONV_K = 31
FFN_K = 3
ROPE_THETA = 10000.0
EPS = 1e-6
NEG = -1e30
SCALE = HEAD_DIM ** -0.5

V7X_LANES = 128
V7X_SUBLANES = 8
V7X_VMEM_LIMIT_BYTES = 56 * 1024 * 1024

MIX_TILE_ROWS = 512
FFN_TILE_ROWS = 1024
FFN_TILE_COLS = 256
OUT_COLS = 256
FFN_DOWN_COLS = 256

CONV_HALO = 16
CONV_ROWS = 32
FFN_HALO = V7X_SUBLANES
NORM_ROWS = 64


def _sigmoid(x):
    return 1.0 / (1.0 + jnp.exp(-x))


def _rms_mod(x, g, sc, sh):
    ms = jnp.mean(x * x, axis=-1, keepdims=True)
    return (x * lax.rsqrt(ms + EPS) * g) * (1.0 + sc) + sh


def _const_spec(shape):
    nd = len(shape)
    return pl.BlockSpec(shape, lambda *_: (0,) * nd, pipeline_mode=pl.Buffered(1))


def _params(sem):
    return pltpu.CompilerParams(dimension_semantics=sem, vmem_limit_bytes=V7X_VMEM_LIMIT_BYTES)


def _ada_kernel(c_ref, w_ref, b_ref, o_ref):
    c = c_ref[...]
    s = (c * _sigmoid(c)).astype(BF16)
    o_ref[0] = jnp.dot(s, w_ref[0].astype(BF16), preferred_element_type=F32) + b_ref[0]


def _ada(cond, w_ada, b_ada):
    depth, d, n = w_ada.shape
    r = cond.shape[0]
    tn = 1024
    return pl.pallas_call(
        _ada_kernel,
        grid=(depth, n // tn),
        in_specs=[
            pl.BlockSpec((r, d), lambda l, j: (0, 0)),
            pl.BlockSpec((1, d, tn), lambda l, j: (l, 0, j)),
            pl.BlockSpec((1, 1, tn), lambda l, j: (l, 0, j)),
        ],
        out_specs=pl.BlockSpec((1, r, tn), lambda l, j: (l, 0, j)),
        out_shape=jax.ShapeDtypeStruct((depth, r, n), F32),
        compiler_params=_params(("arbitrary", "arbitrary")),
        name="ada",
    )(cond, w_ada, b_ada.reshape(depth, 1, n))


def _rope(t, cos, sin):
    outs = []
    lane = lax.broadcasted_iota(jnp.int32, (t.shape[0], V7X_LANES), 1)
    first = (lane % HEAD_DIM) < (HEAD_DIM // 2)
    for j in range(t.shape[1] // V7X_LANES):
        tj = t[:, j * V7X_LANES:(j + 1) * V7X_LANES]
        rot = jnp.where(first, pltpu.roll(tj, V7X_LANES - HEAD_DIM // 2, 1), pltpu.roll(tj, HEAD_DIM // 2, 1))
        outs.append(tj * cos + rot * sin)
    return jnp.concatenate(outs, axis=1)


def _inproj_kernel(*refs, tm, dims, use_rope, emit_kv_f32):
    attn_dim, kv_dim, conv_dim, gmlp_dim = dims
    it = iter(refs)
    x_ref, mod_ref, g_ref, w_ref = next(it), next(it), next(it), next(it)
    if use_rope:
        cos_ref, sin_ref = next(it), next(it)
    lng_ref, lnb_ref, ws_ref, bs_ref = next(it), next(it), next(it), next(it)
    q_ref, k_ref, v_ref, glu_ref, gm_ref = next(it), next(it), next(it), next(it), next(it)
    if emit_kv_f32:
        kf_ref, vf_ref = next(it), next(it)

    h = _rms_mod(x_ref[...], g_ref[...], mod_ref[0, 1:2, :], mod_ref[0, 0:1, :]).astype(BF16)

    def proj(lo, n):
        return jnp.dot(h, w_ref[:, lo:lo + n], preferred_element_type=F32)

    o_k = attn_dim
    o_v = o_k + kv_dim
    o_a = o_v + kv_dim
    o_g = o_a + conv_dim
    o_u = o_g + conv_dim
    o_w = o_u + gmlp_dim

    q = proj(0, attn_dim)
    k = proj(o_k, kv_dim)
    v = proj(o_v, kv_dim)
    if emit_kv_f32:
        kf_ref[...] = k
        vf_ref[...] = v
    if use_rope:
        cos, sin = cos_ref[...], sin_ref[...]
        q = _rope(q, cos, sin)
        k = _rope(k, cos, sin)
    q_ref[...] = q.astype(BF16)
    k_ref[...] = k.astype(BF16)
    v_ref[...] = v.astype(BF16)

    glu_ref[...] = proj(o_a, conv_dim) * _sigmoid(proj(o_g, conv_dim))

    u = jax.nn.gelu(proj(o_u, gmlp_dim))
    w = jax.nn.gelu(proj(o_w, gmlp_dim))
    for g in range(gmlp_dim // GMLP_GROUP_DIM):
        cs = slice(g * GMLP_GROUP_DIM, (g + 1) * GMLP_GROUP_DIM)
        wg = w[:, cs]
        mu = jnp.mean(wg, axis=-1, keepdims=True)
        var = jnp.mean(jnp.square(wg - mu), axis=-1, keepdims=True)
        wn = ((wg - mu) * lax.rsqrt(var + EPS) * lng_ref[:, cs] + lnb_ref[:, cs]).astype(BF16)
        for c in range(tm // CHUNK):
            rs = slice(c * CHUNK, (c + 1) * CHUNK)
            sv = jnp.dot(ws_ref[g], wn[rs, :], preferred_element_type=F32) + bs_ref[:, cs]
            gm_ref[rs, cs] = (u[rs, cs] * sv).astype(BF16)


def _inproj(x, mod, g_norm, w_in, rope, gmlp, *, tm, tiles_per_mod, dims, emit_kv_f32):
    n, d = x.shape
    attn_dim, kv_dim, conv_dim, gmlp_dim = dims
    lng, lnb, ws, bs = gmlp
    row = lambda w: pl.BlockSpec((tm, w), lambda i: (i, 0))
    in_specs = [
        row(d),
        pl.BlockSpec((1, 6, d), lambda i: (i // tiles_per_mod, 0, 0)),
        _const_spec((1, d)),
        _const_spec(w_in.shape),
    ]
    args = [x, mod, g_norm, w_in]
    if rope is not None:
        tiles_per_seq = rope[0].shape[0] // tm
        in_specs += [pl.BlockSpec((tm, V7X_LANES), lambda i: (i % tiles_per_seq, 0))] * 2
        args += list(rope)
    in_specs += [_const_spec(lng.shape), _const_spec(lnb.shape), _const_spec(ws.shape), _const_spec(bs.shape)]
    args += [lng, lnb, ws, bs]
    out_specs = [row(attn_dim), row(kv_dim), row(kv_dim), row(conv_dim), row(gmlp_dim)]
    out_shape = [
        jax.ShapeDtypeStruct((n, attn_dim), BF16),
        jax.ShapeDtypeStruct((n, kv_dim), BF16),
        jax.ShapeDtypeStruct((n, kv_dim), BF16),
        jax.ShapeDtypeStruct((n, conv_dim), F32),
        jax.ShapeDtypeStruct((n, gmlp_dim), BF16),
    ]
    if emit_kv_f32:
        out_specs += [row(kv_dim), row(kv_dim)]
        out_shape += [jax.ShapeDtypeStruct((n, kv_dim), F32)] * 2
    return pl.pallas_call(
        functools.partial(_inproj_kernel, tm=tm, dims=dims, use_rope=rope is not None, emit_kv_f32=emit_kv_f32),
        grid=(n // tm,),
        in_specs=in_specs,
        out_specs=out_specs,
        out_shape=out_shape,
        compiler_params=_params(("arbitrary",)),
        name="inproj",
    )(*args)


def _attn_kernel(*refs, nb, local):
    it = iter(refs)
    sink_ref, q_ref = next(it), next(it)
    if local:
        kl_refs = (next(it), next(it), next(it))
        vl_refs = (next(it), next(it), next(it))
    ck_ref, cv_ref, o_ref = next(it), next(it), next(it)

    rows = Q_PER_KV * BLOCK
    lane = lax.broadcasted_iota(jnp.int32, (rows, V7X_LANES), 1)
    rowg = lax.broadcasted_iota(jnp.int32, (rows, 1), 0) // BLOCK
    upper = lane >= HEAD_DIM
    nt = (((1,), (1,)), ((), ()))

    if local:
        n = pl.program_id(0) % nb
        qi = lax.broadcasted_iota(jnp.int32, (rows, 3 * BLOCK), 0) % BLOCK
        kj = lax.broadcasted_iota(jnp.int32, (rows, 3 * BLOCK), 1)
        lo = jnp.where(n == 0, BLOCK, 0)
        hi = jnp.where(n == nb - 1, 2 * BLOCK, 3 * BLOCK)
        valid = (kj >= qi) & (kj <= qi + 2 * BLOCK) & (kj >= lo) & (kj < hi)

    for p in range(N_KV_HEADS // 2):
        ls = slice(p * V7X_LANES, (p + 1) * V7X_LANES)
        qs = jnp.concatenate(
            [q_ref[:, g * 2 * V7X_LANES + p * V7X_LANES: g * 2 * V7X_LANES + (p + 1) * V7X_LANES]
             for g in range(Q_PER_KV)], axis=0) * SCALE
        ck = ck_ref[0, :, ls]
        cv = cv_ref[0, :, ls]
        if local:
            kl = jnp.concatenate([r[:, ls] for r in kl_refs], axis=0)
            vl = jnp.concatenate([r[:, ls] for r in vl_refs], axis=0)
        outs = []
        for hh in range(2):
            kvh = 2 * p + hh
            qm = jnp.where(upper if hh else jnp.logical_not(upper), qs, jnp.zeros_like(qs))
            sk = jnp.zeros((rows, 1), F32)
            for g in range(Q_PER_KV):
                sk = jnp.where(rowg == g, sink_ref[kvh * Q_PER_KV + g], sk)
            s_ctx = lax.dot_general(qm, ck, nt, preferred_element_type=F32)
            m = jnp.maximum(jnp.max(s_ctx, axis=-1, keepdims=True), sk)
            if local:
                s_loc = lax.dot_general(qm, kl, nt, preferred_element_type=F32)
                s_loc = jnp.where(valid, s_loc, NEG)
                m = jnp.maximum(m, jnp.max(s_loc, axis=-1, keepdims=True))
            p_ctx = jnp.exp(s_ctx - m)
            den = jnp.sum(p_ctx, axis=-1, keepdims=True) + jnp.exp(sk - m)
            acc = jnp.dot(p_ctx.astype(BF16), cv, preferred_element_type=F32)
            if local:
                p_loc = jnp.exp(s_loc - m)
                den = den + jnp.sum(p_loc, axis=-1, keepdims=True)
                acc = acc + jnp.dot(p_loc.astype(BF16), vl, preferred_element_type=F32)
            outs.append(acc / den)
        o = jnp.where(upper, outs[1], outs[0]).astype(BF16)
        for g in range(Q_PER_KV):
            o_ref[:, g * 2 * V7X_LANES + p * V7X_LANES: g * 2 * V7X_LANES + (p + 1) * V7X_LANES] = (
                o[g * BLOCK:(g + 1) * BLOCK, :])


def _attention(q, k, v, ck, cv, sink, *, nb, local):
    n, attn_dim = q.shape
    kv_dim = ck.shape[-1]
    nblk = n // BLOCK
    in_specs = [pl.BlockSpec(memory_space=pltpu.SMEM), pl.BlockSpec((BLOCK, attn_dim), lambda i: (i, 0))]
    args = [sink, q]
    if local:
        maps = [lambda i: (jnp.maximum(i - 1, 0), 0), lambda i: (i, 0), lambda i: (jnp.minimum(i + 1, nblk - 1), 0)]
        in_specs += [pl.BlockSpec((BLOCK, kv_dim), m) for m in maps] * 2
        args += [k, k, k, v, v, v]
    in_specs += [pl.BlockSpec((1,) + ck.shape[1:], lambda i: (i // nb, 0, 0))] * 2
    args += [ck, cv]
    return pl.pallas_call(
        functools.partial(_attn_kernel, nb=nb, local=local),
        grid=(nblk,),
        in_specs=in_specs,
        out_specs=pl.BlockSpec((BLOCK, attn_dim), lambda i: (i, 0)),
        out_shape=jax.ShapeDtypeStruct((n, attn_dim), BF16),
        compiler_params=_params(("arbitrary",)),
        name="attention",
    )(*args)


def _outproj_kernel(attn_ref, glu_ref, glup_ref, glun_ref, gm_ref, x_ref, mod_ref, cw_ref, cb_ref,
                    clg_ref, clb_ref, wo_ref, o_ref, pad_ref, shift_ref, conv_ref, *, tm, seg, tiles_per_seq):
    n = pl.program_id(0) % tiles_per_seq
    attn_dim = attn_ref.shape[1]
    conv_dim = glu_ref.shape[1]
    stride = seg + 2 * CONV_HALO
    total = (tm // seg) * stride
    zeros = jnp.zeros((CONV_HALO, conv_dim), F32)

    for s in range(tm // seg):
        base = s * stride
        lead, trail = zeros, zeros
        if tiles_per_seq > 1:
            lead = jnp.where(n > 0, glup_ref[...], 0.0)
            trail = jnp.where(n < tiles_per_seq - 1, glun_ref[...], 0.0)
        pad_ref[base:base + CONV_HALO, :] = lead
        pad_ref[base + CONV_HALO:base + CONV_HALO + seg, :] = glu_ref[s * seg:(s + 1) * seg, :]
        pad_ref[base + CONV_HALO + seg:base + stride, :] = trail
    for b in range(1, V7X_SUBLANES):
        shift_ref[b - 1, 0:total - V7X_SUBLANES, :] = pad_ref[b:b + total - V7X_SUBLANES, :]

    first_tap = CONV_HALO - (CONV_K - 1) // 2
    groups = CONV_ROWS // V7X_SUBLANES

    def conv_block(s, r0):
        accs = [jnp.zeros((V7X_SUBLANES, conv_dim), F32) + cb_ref[...]] * groups
        for j in range(CONV_K):
            off = first_tap + j
            lo = s * stride + r0 + (off // V7X_SUBLANES) * V7X_SUBLANES
            sub = off % V7X_SUBLANES
            w = cw_ref[j]
            for q in range(groups):
                rs = slice(lo + q * V7X_SUBLANES, lo + (q + 1) * V7X_SUBLANES)
                win = pad_ref[rs, :] if sub == 0 else shift_ref[sub - 1, rs, :]
                accs[q] = accs[q] + win * w
        acc = jnp.concatenate(accs, axis=0)
        mu = jnp.mean(acc, axis=-1, keepdims=True)
        var = jnp.mean(jnp.square(acc - mu), axis=-1, keepdims=True)
        y = (acc - mu) * lax.rsqrt(var + EPS) * clg_ref[...] + clb_ref[...]
        conv_ref[s * seg + r0:s * seg + r0 + CONV_ROWS, :] = (y * _sigmoid(y)).astype(BF16)

    blocks = [(s, r0) for s in range(tm // seg) for r0 in range(0, seg, CONV_ROWS)]
    n_col_tiles = o_ref.shape[1] // OUT_COLS
    per_tile = -(-len(blocks) // n_col_tiles)
    for k in range(n_col_tiles):
        cols = slice(k * OUT_COLS, (k + 1) * OUT_COLS)
        part = jnp.dot(attn_ref[...], wo_ref[0:attn_dim, cols], preferred_element_type=F32)
        o_ref[:, cols] = part + jnp.dot(gm_ref[...], wo_ref[attn_dim + conv_dim:, cols], preferred_element_type=F32)
        for blk in blocks[k * per_tile:(k + 1) * per_tile]:
            conv_block(*blk)
    g1 = mod_ref[0, 2:3, :]
    for k in range(n_col_tiles):
        cols = slice(k * OUT_COLS, (k + 1) * OUT_COLS)
        part = jnp.dot(conv_ref[...], wo_ref[attn_dim:attn_dim + conv_dim, cols], preferred_element_type=F32)
        o_ref[:, cols] = x_ref[:, cols] + g1[:, cols] * (o_ref[:, cols] + part)


def _outproj(attn, glu, gm, x, mod, conv, w_out, *, tm, seg, tiles_per_seq, tiles_per_mod):
    n, d = x.shape
    cw, cb, clg, clb = conv
    conv_dim = glu.shape[1]
    hb = tm // CONV_HALO
    nhb = n // CONV_HALO
    pad_rows = (tm // seg) * (seg + 2 * CONV_HALO)
    row = lambda w: pl.BlockSpec((tm, w), lambda i: (i, 0))
    return pl.pallas_call(
        functools.partial(_outproj_kernel, tm=tm, seg=seg, tiles_per_seq=tiles_per_seq),
        grid=(n // tm,),
        in_specs=[
            row(attn.shape[1]),
            row(conv_dim),
            pl.BlockSpec((CONV_HALO, conv_dim), lambda i: (jnp.maximum(i * hb - 1, 0), 0)),
            pl.BlockSpec((CONV_HALO, conv_dim), lambda i: (jnp.minimum((i + 1) * hb, nhb - 1), 0)),
            row(gm.shape[1]),
            row(d),
            pl.BlockSpec((1, 6, d), lambda i: (i // tiles_per_mod, 0, 0)),
            _const_spec(cw.shape), _const_spec(cb.shape), _const_spec(clg.shape), _const_spec(clb.shape),
            _const_spec(w_out.shape),
        ],
        out_specs=row(d),
        out_shape=jax.ShapeDtypeStruct((n, d), F32),
        scratch_shapes=[
            pltpu.VMEM((pad_rows, conv_dim), F32),
            pltpu.VMEM((V7X_SUBLANES - 1, pad_rows, conv_dim), F32),
            pltpu.VMEM((tm, conv_dim), BF16),
        ],
        compiler_params=_params(("arbitrary",)),
        name="outproj",
    )(attn, glu, glu, glu, gm, x, mod, cw, cb, clg, clb, w_out)


def _ffn_kernel(*refs, tm, seg, tiles_per_seq, nf, final):
    it = iter(refs)
    x_ref, xp_ref, xn_ref, mod_ref, g_ref = next(it), next(it), next(it), next(it), next(it)
    wa_ref, wb_ref, cwa_ref, cwb_ref, cba_ref, cbb_ref, wd_ref = (next(it) for _ in range(7))
    if final:
        gf_ref = next(it)
    o_ref, h_ref, hh_ref, u_ref, uh_ref, act0_ref, act1_ref = (next(it) for _ in range(7))
    act_refs = (act0_ref, act1_ref)

    f = pl.program_id(1)
    n = pl.program_id(0) % tiles_per_seq
    d = x_ref.shape[1]
    tf = wa_ref.shape[1]
    n_col_tiles = d // FFN_DOWN_COLS

    def up():
        h = h_ref[...]
        hh = hh_ref[...]
        for j, w_ref in enumerate((wa_ref, wb_ref)):
            u_ref[:, j * tf:(j + 1) * tf] = jnp.dot(h, w_ref[...], preferred_element_type=F32)
            uh_ref[:, j * tf:(j + 1) * tf] = jnp.dot(hh, w_ref[...], preferred_element_type=F32)

    def gate_rows(r0, r1, act_ref):
        rows = r1 - r0
        zeros = jnp.zeros((FFN_HALO, tf), F32)
        mid = slice(FFN_HALO, FFN_HALO + rows)
        row = lax.broadcasted_iota(jnp.int32, (rows, 1), 0) + r0
        inner = [b for b in range(seg, tm, seg) if r0 < b < r1]

        def conv(j, cw_ref, cb_ref):
            cols = slice(j * tf, (j + 1) * tf)
            if r0 == 0:
                top = uh_ref[0:FFN_HALO, cols]
            elif r0 % seg == 0:
                top = zeros
            else:
                top = u_ref[r0 - FFN_HALO:r0, cols]
            if r1 == tm:
                bot = uh_ref[FFN_HALO:, cols]
            elif r1 % seg == 0:
                bot = zeros
            else:
                bot = u_ref[r1:r1 + FFN_HALO, cols]
            ext = jnp.concatenate([top, u_ref[r0:r1, cols], bot], axis=0)
            prev = pltpu.roll(ext, 1, 0)[mid]
            nxt = pltpu.roll(ext, rows + 2 * FFN_HALO - 1, 0)[mid]
            for b in inner:
                prev = jnp.where(row == b, 0.0, prev)
                nxt = jnp.where(row == b - 1, 0.0, nxt)
            return cw_ref[0:1, :] * prev + cw_ref[1:2, :] * ext[mid] + cw_ref[2:3, :] * nxt + cb_ref[...]

        a = conv(0, cwa_ref, cba_ref)
        b = conv(1, cwb_ref, cbb_ref)
        act_ref[r0:r1, :] = (a * _sigmoid(a) * b).astype(BF16)

    def down_cols(k, act_ref):
        cols = slice(k * FFN_DOWN_COLS, (k + 1) * FFN_DOWN_COLS)
        return cols, jnp.dot(act_ref[...], wd_ref[:, cols], preferred_element_type=F32)

    def step(act_new, act_old):
        up()
        chunk = tm // n_col_tiles
        for k in range(n_col_tiles):
            if act_old is not None:
                cols, part = down_cols(k, act_old)
                o_ref[:, cols] += part
            gate_rows(k * chunk, (k + 1) * chunk, act_new)

    @pl.when(f == 0)
    def _():
        g, sc, sh = g_ref[...], mod_ref[0, 4:5, :], mod_ref[0, 3:4, :]

        gs = g * (1.0 + sc)

        def norm_rows(r, carry):
            rs = pl.ds(pl.multiple_of(r * NORM_ROWS, NORM_ROWS), NORM_ROWS)
            x = x_ref[rs, :]
            ms = jnp.mean(x * x, axis=-1, keepdims=True)
            h_ref[rs, :] = (x * lax.rsqrt(ms + EPS) * gs + sh).astype(BF16)
            return carry

        lax.fori_loop(0, tm // NORM_ROWS, norm_rows, 0)
        hp = jnp.where(n > 0, _rms_mod(xp_ref[...], g, sc, sh), 0.0)
        hn = jnp.where(n < tiles_per_seq - 1, _rms_mod(xn_ref[...], g, sc, sh), 0.0)
        hh_ref[0:FFN_HALO, :] = hp.astype(BF16)
        hh_ref[FFN_HALO:, :] = hn.astype(BF16)
        o_ref[...] = jnp.zeros((tm, d), F32)
        step(act_refs[0], None)

    for parity in range(2):
        @pl.when((f > 0) & (f < nf) & (f % 2 == parity))
        def _():
            step(act_refs[parity], act_refs[1 - parity])

    @pl.when(f == nf)
    def _():
        g2 = mod_ref[0, 5:6, :]
        for k in range(n_col_tiles):
            cols, part = down_cols(k, act_refs[(nf - 1) % 2])
            o_ref[:, cols] = x_ref[:, cols] + g2[:, cols] * (o_ref[:, cols] + part)
        if final:
            y = o_ref[...]
            ms = jnp.mean(y * y, axis=-1, keepdims=True)
            o_ref[...] = y * lax.rsqrt(ms + EPS) * gf_ref[...]


def _ffn(x, mod, g_norm, w_up, cw, cb, w_down, g_final, *, tm, tf, seg, tiles_per_seq, tiles_per_mod):
    n, d = x.shape
    d_ff = w_down.shape[0]
    nf = d_ff // tf
    hb = tm // FFN_HALO
    nhb = n // FFN_HALO
    final = g_final is not None
    up_chunk = lambda i, f: jnp.minimum(f, nf - 1)
    in_specs = [
        pl.BlockSpec((tm, d), lambda i, f: (i, 0)),
        pl.BlockSpec((FFN_HALO, d), lambda i, f: (jnp.maximum(i * hb - 1, 0), 0)),
        pl.BlockSpec((FFN_HALO, d), lambda i, f: (jnp.minimum((i + 1) * hb, nhb - 1), 0)),
        pl.BlockSpec((1, 6, d), lambda i, f: (i // tiles_per_mod, 0, 0)),
        _const_spec((1, d)),
        pl.BlockSpec((d, tf), lambda i, f: (0, up_chunk(i, f))),
        pl.BlockSpec((d, tf), lambda i, f: (0, nf + up_chunk(i, f))),
        pl.BlockSpec((FFN_K, tf), lambda i, f: (0, up_chunk(i, f))),
        pl.BlockSpec((FFN_K, tf), lambda i, f: (0, nf + up_chunk(i, f))),
        pl.BlockSpec((1, tf), lambda i, f: (0, up_chunk(i, f))),
        pl.BlockSpec((1, tf), lambda i, f: (0, nf + up_chunk(i, f))),
        pl.BlockSpec((tf, d), lambda i, f: (jnp.maximum(f - 1, 0), 0)),
    ]
    args = [x, x, x, mod, g_norm, w_up, w_up, cw, cw, cb, cb, w_down]
    if final:
        in_specs.append(_const_spec((1, d)))
        args.append(g_final)
    return pl.pallas_call(
        functools.partial(_ffn_kernel, tm=tm, seg=seg, tiles_per_seq=tiles_per_seq, nf=nf, final=final),
        grid=(n // tm, nf + 1),
        in_specs=in_specs,
        out_specs=pl.BlockSpec((tm, d), lambda i, f: (i, 0)),
        out_shape=jax.ShapeDtypeStruct((n, d), F32),
        scratch_shapes=[pltpu.VMEM((tm, d), BF16), pltpu.VMEM((2 * FFN_HALO, d), BF16),
                        pltpu.VMEM((tm, 2 * tf), F32), pltpu.VMEM((2 * FFN_HALO, 2 * tf), F32),
                        pltpu.VMEM((tm, tf), BF16), pltpu.VMEM((tm, tf), BF16)],
        compiler_params=_params(("arbitrary", "arbitrary")),
        name="ffn",
    )(*args)


def _rope_tables(seq_len):
    rows = seq_len // GRID_W
    row = jnp.repeat(jnp.arange(rows, dtype=F32), GRID_W)
    col = jnp.tile(jnp.arange(GRID_W, dtype=F32), rows)
    n_freq = HEAD_DIM // 4
    inv = ROPE_THETA ** (-jnp.arange(n_freq, dtype=F32) / n_freq)
    ang = jnp.concatenate([row[:, None] * inv, col[:, None] * inv], axis=-1)
    cos, sin = jnp.cos(ang), jnp.sin(ang)
    reps = V7X_LANES // HEAD_DIM
    return jnp.tile(jnp.concatenate([cos, cos], -1), (1, reps)), jnp.tile(jnp.concatenate([-sin, sin], -1), (1, reps))


def _gmajor_cols(w, d_in):
    return w.reshape(d_in, N_KV_HEADS, Q_PER_KV, HEAD_DIM).transpose(0, 2, 1, 3).reshape(d_in, -1)


def kernel(x_prompt, x_sample, cache_k, cache_v, c, c_ctx, w_ada, b_ada, g_norm1, w_in, attn_sink, conv_dw_w, conv_dw_b, conv_ln_g, conv_ln_b, gmlp_ln_g, gmlp_ln_b, gmlp_ws, gmlp_bs, w_out, g_norm2, w_up, ffn_dw_w, ffn_dw_b, w_down, g_final):
    b1, l1, d = x_prompt.shape
    b2, l2, _ = x_sample.shape
    depth = w_ada.shape[0]
    past = cache_k.shape[2]
    kv_dim = N_KV_HEADS * HEAD_DIM
    attn_dim = N_KV_HEADS * Q_PER_KV * HEAD_DIM
    conv_dim = conv_dw_w.shape[-1]
    gmlp_dim = gmlp_ln_g.shape[1] * gmlp_ln_g.shape[2]
    dims = (attn_dim, kv_dim, conv_dim, gmlp_dim)

    n_cond = 1 + b2
    cond = jnp.concatenate([c_ctx[None, :], c], axis=0)
    cond = jnp.pad(cond, ((0, (-n_cond) % V7X_SUBLANES), (0, 0)))
    mod_all = _ada(cond, w_ada, b_ada)

    rope = _rope_tables(l2)
    xp = x_prompt.reshape(b1 * l1, d)
    xs = x_sample.reshape(b2 * l2, d)
    new_k, new_v = [], []

    for l in range(depth):
        w_in_l = jnp.concatenate([_gmajor_cols(w_in[l, :, :attn_dim], d), w_in[l, :, attn_dim:]], axis=1).astype(BF16)
        w_out_l = jnp.concatenate([_gmajor_cols(w_out[l, :attn_dim].T, d).T, w_out[l, attn_dim:]], axis=0).astype(BF16)
        w_up_l = w_up[l].astype(BF16)
        w_down_l = w_down[l].astype(BF16)
        gmlp = (gmlp_ln_g[l].reshape(1, gmlp_dim), gmlp_ln_b[l].reshape(1, gmlp_dim), gmlp_ws[l].astype(BF16),
                jnp.repeat(gmlp_bs[l].T, GMLP_GROUP_DIM, axis=1))
        conv = (jnp.broadcast_to(conv_dw_w[l][:, None, :], (CONV_K, V7X_SUBLANES, conv_dim)), conv_dw_b[l][None, :],
                conv_ln_g[l][None, :], conv_ln_b[l][None, :])
        mod = mod_all[l].reshape(-1, 6, d)
        g_last = g_final[None, :] if l == depth - 1 else None

        def layer(x, seq_len, mod_rows, per_seq_mod, rope_l, window_kv, ctx_kv, emit_kv):
            def tiling(tm):
                tps = max(seq_len // tm, 1)
                return min(tm, seq_len), tps, tps if per_seq_mod else x.shape[0] // tm

            tm = MIX_TILE_ROWS
            seg, tps, tpm = tiling(tm)
            outs = _inproj(x, mod_rows, g_norm1[l][None, :], w_in_l, rope_l, gmlp, tm=tm, tiles_per_mod=tpm,
                           dims=dims, emit_kv_f32=emit_kv)
            q, k, v, glu, gm = outs[:5]
            if window_kv:
                attn = _attention(q, k, v, ctx_kv[0], ctx_kv[1], attn_sink[l], nb=seq_len // BLOCK, local=True)
            else:
                nseq = x.shape[0] // seq_len
                attn = _attention(q, None, None, k.reshape(nseq, seq_len, kv_dim), v.reshape(nseq, seq_len, kv_dim),
                                  attn_sink[l], nb=seq_len // BLOCK, local=False)
            x = _outproj(attn, glu, gm, x, mod_rows, conv, w_out_l, tm=tm, seg=seg, tiles_per_seq=tps,
                         tiles_per_mod=tpm)
            tmf = FFN_TILE_ROWS
            segf, tpsf, tpmf = tiling(tmf)
            x = _ffn(x, mod_rows, g_norm2[l][None, :], w_up_l, ffn_dw_w[l], ffn_dw_b[l][None, :], w_down_l, g_last,
                     tm=tmf, tf=FFN_TILE_COLS, seg=segf, tiles_per_seq=tpsf, tiles_per_mod=tpmf)
            return x, outs[5:]

        xp, kv_f32 = layer(xp, l1, mod[0:1], False, None, False, None, True)
        new_k.append(kv_f32[0].reshape(b1, l1, N_KV_HEADS, HEAD_DIM))
        new_v.append(kv_f32[1].reshape(b1, l1, N_KV_HEADS, HEAD_DIM))
        ctx_kv = (cache_k[:, l].reshape(b2, past, kv_dim).astype(BF16), cache_v[:, l].reshape(b2, past, kv_dim).astype(BF16))
        xs, _ = layer(xs, l2, mod[1:1 + b2], True, rope, True, ctx_kv, False)

    return (xp.reshape(b1, l1, d), xs.reshape(b2, l2, d), jnp.stack(new_k, axis=1), jnp.stack(new_v, axis=1))
```

```python
import functools

import jax
import jax.numpy as jnp
from jax import lax
from jax.experimental import pallas as pl
from jax.experimental.pallas import tpu as pltpu

F32 = jnp.float32
BF16 = jnp.bfloat16

HEAD_DIM = 64
N_KV_HEADS = 4
Q_PER_KV = 4
GRID_W = 64
BLOCK = 128
CHUNK = 128
GMLP_GROUP_DIM = 128
CONV_K = 31
FFN_K = 3
ROPE_THETA = 10000.0
EPS = 1e-6
NEG = -1e30
SCALE = HEAD_DIM ** -0.5

V7X_LANES = 128
V7X_SUBLANES = 8
V7X_VMEM_LIMIT_BYTES = 56 * 1024 * 1024

MIX_TILE_ROWS = 512
FFN_TILE_ROWS = 1024
FFN_TILE_COLS = 512
FFN_SUB_COLS = 256
FFN_GATE_CHUNKS = 8
OUT_COLS = 256

CONV_HALO = 16
CONV_ROWS = 32
FFN_HALO = V7X_SUBLANES
NORM_ROWS = 64


def _sigmoid(x):
    return 1.0 / (1.0 + jnp.exp(-x))


def _rms_mod(x, g, sc, sh):
    ms = jnp.mean(x * x, axis=-1, keepdims=True)
    return (x * lax.rsqrt(ms + EPS) * g) * (1.0 + sc) + sh


def _const_spec(shape):
    nd = len(shape)
    return pl.BlockSpec(shape, lambda *_: (0,) * nd, pipeline_mode=pl.Buffered(1))


def _params(sem):
    return pltpu.CompilerParams(dimension_semantics=sem, vmem_limit_bytes=V7X_VMEM_LIMIT_BYTES)


def _ada_kernel(c_ref, w_ref, b_ref, o_ref):
    c = c_ref[...]
    s = (c * _sigmoid(c)).astype(BF16)
    o_ref[0] = jnp.dot(s, w_ref[0].astype(BF16), preferred_element_type=F32) + b_ref[0]


def _ada(cond, w_ada, b_ada):
    depth, d, n = w_ada.shape
    r = cond.shape[0]
    tn = 1024
    return pl.pallas_call(
        _ada_kernel,
        grid=(depth, n // tn),
        in_specs=[
            pl.BlockSpec((r, d), lambda l, j: (0, 0)),
            pl.BlockSpec((1, d, tn), lambda l, j: (l, 0, j)),
            pl.BlockSpec((1, 1, tn), lambda l, j: (l, 0, j)),
        ],
        out_specs=pl.BlockSpec((1, r, tn), lambda l, j: (l, 0, j)),
        out_shape=jax.ShapeDtypeStruct((depth, r, n), F32),
        compiler_params=_params(("arbitrary", "arbitrary")),
        name="ada",
    )(cond, w_ada, b_ada.reshape(depth, 1, n))


def _rope(t, cos, sin):
    outs = []
    lane = lax.broadcasted_iota(jnp.int32, (t.shape[0], V7X_LANES), 1)
    first = (lane % HEAD_DIM) < (HEAD_DIM // 2)
    for j in range(t.shape[1] // V7X_LANES):
        tj = t[:, j * V7X_LANES:(j + 1) * V7X_LANES]
        rot = jnp.where(first, pltpu.roll(tj, V7X_LANES - HEAD_DIM // 2, 1), pltpu.roll(tj, HEAD_DIM // 2, 1))
        outs.append(tj * cos + rot * sin)
    return jnp.concatenate(outs, axis=1)


def _inproj_kernel(*refs, tm, dims, use_rope, emit_kv_f32):
    attn_dim, kv_dim, conv_dim, gmlp_dim = dims
    it = iter(refs)
    x_ref, mod_ref, g_ref, w_ref = next(it), next(it), next(it), next(it)
    if use_rope:
        cos_ref, sin_ref = next(it), next(it)
    lng_ref, lnb_ref, ws_ref, bs_ref = next(it), next(it), next(it), next(it)
    q_ref, k_ref, v_ref, glu_ref, gm_ref = next(it), next(it), next(it), next(it), next(it)
    if emit_kv_f32:
        kf_ref, vf_ref = next(it), next(it)

    h = _rms_mod(x_ref[...], g_ref[...], mod_ref[0, 1:2, :], mod_ref[0, 0:1, :]).astype(BF16)

    def proj(lo, n):
        return jnp.dot(h, w_ref[:, lo:lo + n], preferred_element_type=F32)

    o_k = attn_dim
    o_v = o_k + kv_dim
    o_a = o_v + kv_dim
    o_g = o_a + conv_dim
    o_u = o_g + conv_dim
    o_w = o_u + gmlp_dim

    q = proj(0, attn_dim)
    k = proj(o_k, kv_dim)
    v = proj(o_v, kv_dim)
    if emit_kv_f32:
        kf_ref[...] = k
        vf_ref[...] = v
    if use_rope:
        cos, sin = cos_ref[...], sin_ref[...]
        q = _rope(q, cos, sin)
        k = _rope(k, cos, sin)
    q_ref[...] = q.astype(BF16)
    k_ref[...] = k.astype(BF16)
    v_ref[...] = v.astype(BF16)

    glu_ref[...] = proj(o_a, conv_dim) * _sigmoid(proj(o_g, conv_dim))

    u = jax.nn.gelu(proj(o_u, gmlp_dim))
    w = jax.nn.gelu(proj(o_w, gmlp_dim))
    for g in range(gmlp_dim // GMLP_GROUP_DIM):
        cs = slice(g * GMLP_GROUP_DIM, (g + 1) * GMLP_GROUP_DIM)
        wg = w[:, cs]
        mu = jnp.mean(wg, axis=-1, keepdims=True)
        var = jnp.mean(jnp.square(wg - mu), axis=-1, keepdims=True)
        wn = ((wg - mu) * lax.rsqrt(var + EPS) * lng_ref[:, cs] + lnb_ref[:, cs]).astype(BF16)
        for c in range(tm // CHUNK):
            rs = slice(c * CHUNK, (c + 1) * CHUNK)
            sv = jnp.dot(ws_ref[g], wn[rs, :], preferred_element_type=F32) + bs_ref[:, cs]
            gm_ref[rs, cs] = (u[rs, cs] * sv).astype(BF16)


def _inproj(x, mod, g_norm, w_in, rope, gmlp, *, tm, tiles_per_mod, dims, emit_kv_f32):
    n, d = x.shape
    attn_dim, kv_dim, conv_dim, gmlp_dim = dims
    lng, lnb, ws, bs = gmlp
    row = lambda w: pl.BlockSpec((tm, w), lambda i: (i, 0))
    in_specs = [
        row(d),
        pl.BlockSpec((1, 6, d), lambda i: (i // tiles_per_mod, 0, 0)),
        _const_spec((1, d)),
        _const_spec(w_in.shape),
    ]
    args = [x, mod, g_norm, w_in]
    if rope is not None:
        tiles_per_seq = rope[0].shape[0] // tm
        in_specs += [pl.BlockSpec((tm, V7X_LANES), lambda i: (i % tiles_per_seq, 0))] * 2
        args += list(rope)
    in_specs += [_const_spec(lng.shape), _const_spec(lnb.shape), _const_spec(ws.shape), _const_spec(bs.shape)]
    args += [lng, lnb, ws, bs]
    out_specs = [row(attn_dim), row(kv_dim), row(kv_dim), row(conv_dim), row(gmlp_dim)]
    out_shape = [
        jax.ShapeDtypeStruct((n, attn_dim), BF16),
        jax.ShapeDtypeStruct((n, kv_dim), BF16),
        jax.ShapeDtypeStruct((n, kv_dim), BF16),
        jax.ShapeDtypeStruct((n, conv_dim), F32),
        jax.ShapeDtypeStruct((n, gmlp_dim), BF16),
    ]
    if emit_kv_f32:
        out_specs += [row(kv_dim), row(kv_dim)]
        out_shape += [jax.ShapeDtypeStruct((n, kv_dim), F32)] * 2
    return pl.pallas_call(
        functools.partial(_inproj_kernel, tm=tm, dims=dims, use_rope=rope is not None, emit_kv_f32=emit_kv_f32),
        grid=(n // tm,),
        in_specs=in_specs,
        out_specs=out_specs,
        out_shape=out_shape,
        compiler_params=_params(("arbitrary",)),
        name="inproj",
    )(*args)


def _attn_kernel(*refs, nb, local):
    it = iter(refs)
    sink_ref, q_ref = next(it), next(it)
    if local:
        kl_refs = (next(it), next(it), next(it))
        vl_refs = (next(it), next(it), next(it))
    ck_ref, cv_ref, o_ref = next(it), next(it), next(it)

    rows = Q_PER_KV * BLOCK
    lane = lax.broadcasted_iota(jnp.int32, (rows, V7X_LANES), 1)
    rowg = lax.broadcasted_iota(jnp.int32, (rows, 1), 0) // BLOCK
    upper = lane >= HEAD_DIM
    nt = (((1,), (1,)), ((), ()))

    if local:
        n = pl.program_id(0) % nb
        qi = lax.broadcasted_iota(jnp.int32, (rows, 3 * BLOCK), 0) % BLOCK
        kj = lax.broadcasted_iota(jnp.int32, (rows, 3 * BLOCK), 1)
        lo = jnp.where(n == 0, BLOCK, 0)
        hi = jnp.where(n == nb - 1, 2 * BLOCK, 3 * BLOCK)
        valid = (kj >= qi) & (kj <= qi + 2 * BLOCK) & (kj >= lo) & (kj < hi)

    for p in range(N_KV_HEADS // 2):
        ls = slice(p * V7X_LANES, (p + 1) * V7X_LANES)
        qs = jnp.concatenate(
            [q_ref[:, g * 2 * V7X_LANES + p * V7X_LANES: g * 2 * V7X_LANES + (p + 1) * V7X_LANES]
             for g in range(Q_PER_KV)], axis=0) * SCALE
        ck = ck_ref[0, :, ls]
        cv = cv_ref[0, :, ls]
        if local:
            kl = jnp.concatenate([r[:, ls] for r in kl_refs], axis=0)
            vl = jnp.concatenate([r[:, ls] for r in vl_refs], axis=0)
        outs = []
        for hh in range(2):
            kvh = 2 * p + hh
            qm = jnp.where(upper if hh else jnp.logical_not(upper), qs, jnp.zeros_like(qs))
            sk = jnp.zeros((rows, 1), F32)
            for g in range(Q_PER_KV):
                sk = jnp.where(rowg == g, sink_ref[kvh * Q_PER_KV + g], sk)
            s_ctx = lax.dot_general(qm, ck, nt, preferred_element_type=F32)
            m = jnp.maximum(jnp.max(s_ctx, axis=-1, keepdims=True), sk)
            if local:
                s_loc = lax.dot_general(qm, kl, nt, preferred_element_type=F32)
                s_loc = jnp.where(valid, s_loc, NEG)
                m = jnp.maximum(m, jnp.max(s_loc, axis=-1, keepdims=True))
            p_ctx = jnp.exp(s_ctx - m)
            den = jnp.sum(p_ctx, axis=-1, keepdims=True) + jnp.exp(sk - m)
            acc = jnp.dot(p_ctx.astype(BF16), cv, preferred_element_type=F32)
            if local:
                p_loc = jnp.exp(s_loc - m)
                den = den + jnp.sum(p_loc, axis=-1, keepdims=True)
                acc = acc + jnp.dot(p_loc.astype(BF16), vl, preferred_element_type=F32)
            outs.append(acc / den)
        o = jnp.where(upper, outs[1], outs[0]).astype(BF16)
        for g in range(Q_PER_KV):
            o_ref[:, g * 2 * V7X_LANES + p * V7X_LANES: g * 2 * V7X_LANES + (p + 1) * V7X_LANES] = (
                o[g * BLOCK:(g + 1) * BLOCK, :])


def _attention(q, k, v, ck, cv, sink, *, nb, local):
    n, attn_dim = q.shape
    kv_dim = ck.shape[-1]
    nblk = n // BLOCK
    in_specs = [pl.BlockSpec(memory_space=pltpu.SMEM), pl.BlockSpec((BLOCK, attn_dim), lambda i: (i, 0))]
    args = [sink, q]
    if local:
        maps = [lambda i: (jnp.maximum(i - 1, 0), 0), lambda i: (i, 0), lambda i: (jnp.minimum(i + 1, nblk - 1), 0)]
        in_specs += [pl.BlockSpec((BLOCK, kv_dim), m) for m in maps] * 2
        args += [k, k, k, v, v, v]
    in_specs += [pl.BlockSpec((1,) + ck.shape[1:], lambda i: (i // nb, 0, 0))] * 2
    args += [ck, cv]
    return pl.pallas_call(
        functools.partial(_attn_kernel, nb=nb, local=local),
        grid=(nblk,),
        in_specs=in_specs,
        out_specs=pl.BlockSpec((BLOCK, attn_dim), lambda i: (i, 0)),
        out_shape=jax.ShapeDtypeStruct((n, attn_dim), BF16),
        compiler_params=_params(("arbitrary",)),
        name="attention",
    )(*args)


def _outproj_kernel(attn_ref, glu_ref, glup_ref, glun_ref, gm_ref, x_ref, mod_ref, cw_ref, cb_ref,
                    clg_ref, clb_ref, wo_ref, o_ref, pad_ref, shift_ref, conv_ref, *, tm, seg, tiles_per_seq):
    n = pl.program_id(0) % tiles_per_seq
    attn_dim = attn_ref.shape[1]
    conv_dim = glu_ref.shape[1]
    stride = seg + 2 * CONV_HALO
    total = (tm // seg) * stride
    zeros = jnp.zeros((CONV_HALO, conv_dim), F32)

    for s in range(tm // seg):
        base = s * stride
        lead, trail = zeros, zeros
        if tiles_per_seq > 1:
            lead = jnp.where(n > 0, glup_ref[...], 0.0)
            trail = jnp.where(n < tiles_per_seq - 1, glun_ref[...], 0.0)
        pad_ref[base:base + CONV_HALO, :] = lead
        pad_ref[base + CONV_HALO:base + CONV_HALO + seg, :] = glu_ref[s * seg:(s + 1) * seg, :]
        pad_ref[base + CONV_HALO + seg:base + stride, :] = trail
    for b in range(1, V7X_SUBLANES):
        shift_ref[b - 1, 0:total - V7X_SUBLANES, :] = pad_ref[b:b + total - V7X_SUBLANES, :]

    first_tap = CONV_HALO - (CONV_K - 1) // 2
    groups = CONV_ROWS // V7X_SUBLANES

    def conv_block(s, r0):
        accs = [jnp.zeros((V7X_SUBLANES, conv_dim), F32) + cb_ref[...]] * groups
        for j in range(CONV_K):
            off = first_tap + j
            lo = s * stride + r0 + (off // V7X_SUBLANES) * V7X_SUBLANES
            sub = off % V7X_SUBLANES
            w = cw_ref[j]
            for q in range(groups):
                rs = slice(lo + q * V7X_SUBLANES, lo + (q + 1) * V7X_SUBLANES)
                win = pad_ref[rs, :] if sub == 0 else shift_ref[sub - 1, rs, :]
                accs[q] = accs[q] + win * w
        acc = jnp.concatenate(accs, axis=0)
        mu = jnp.mean(acc, axis=-1, keepdims=True)
        var = jnp.mean(jnp.square(acc - mu), axis=-1, keepdims=True)
        y = (acc - mu) * lax.rsqrt(var + EPS) * clg_ref[...] + clb_ref[...]
        conv_ref[s * seg + r0:s * seg + r0 + CONV_ROWS, :] = (y * _sigmoid(y)).astype(BF16)

    blocks = [(s, r0) for s in range(tm // seg) for r0 in range(0, seg, CONV_ROWS)]
    n_col_tiles = o_ref.shape[1] // OUT_COLS
    per_tile = -(-len(blocks) // n_col_tiles)
    for k in range(n_col_tiles):
        cols = slice(k * OUT_COLS, (k + 1) * OUT_COLS)
        part = jnp.dot(attn_ref[...], wo_ref[0:attn_dim, cols], preferred_element_type=F32)
        o_ref[:, cols] = part + jnp.dot(gm_ref[...], wo_ref[attn_dim + conv_dim:, cols], preferred_element_type=F32)
        for blk in blocks[k * per_tile:(k + 1) * per_tile]:
            conv_block(*blk)
    g1 = mod_ref[0, 2:3, :]
    for k in range(n_col_tiles):
        cols = slice(k * OUT_COLS, (k + 1) * OUT_COLS)
        part = jnp.dot(conv_ref[...], wo_ref[attn_dim:attn_dim + conv_dim, cols], preferred_element_type=F32)
        o_ref[:, cols] = x_ref[:, cols] + g1[:, cols] * (o_ref[:, cols] + part)


def _outproj(attn, glu, gm, x, mod, conv, w_out, *, tm, seg, tiles_per_seq, tiles_per_mod):
    n, d = x.shape
    cw, cb, clg, clb = conv
    conv_dim = glu.shape[1]
    hb = tm // CONV_HALO
    nhb = n // CONV_HALO
    pad_rows = (tm // seg) * (seg + 2 * CONV_HALO)
    row = lambda w: pl.BlockSpec((tm, w), lambda i: (i, 0))
    return pl.pallas_call(
        functools.partial(_outproj_kernel, tm=tm, seg=seg, tiles_per_seq=tiles_per_seq),
        grid=(n // tm,),
        in_specs=[
            row(attn.shape[1]),
            row(conv_dim),
            pl.BlockSpec((CONV_HALO, conv_dim), lambda i: (jnp.maximum(i * hb - 1, 0), 0)),
            pl.BlockSpec((CONV_HALO, conv_dim), lambda i: (jnp.minimum((i + 1) * hb, nhb - 1), 0)),
            row(gm.shape[1]),
            row(d),
            pl.BlockSpec((1, 6, d), lambda i: (i // tiles_per_mod, 0, 0)),
            _const_spec(cw.shape), _const_spec(cb.shape), _const_spec(clg.shape), _const_spec(clb.shape),
            _const_spec(w_out.shape),
        ],
        out_specs=row(d),
        out_shape=jax.ShapeDtypeStruct((n, d), F32),
        scratch_shapes=[
            pltpu.VMEM((pad_rows, conv_dim), F32),
            pltpu.VMEM((V7X_SUBLANES - 1, pad_rows, conv_dim), F32),
            pltpu.VMEM((tm, conv_dim), BF16),
        ],
        compiler_params=_params(("arbitrary",)),
        name="outproj",
    )(attn, glu, glu, glu, gm, x, mod, cw, cb, clg, clb, w_out)


def _ffn_kernel(*refs, tm, seg, tiles_per_seq, nf, final):
    it = iter(refs)
    zero_ref, x_ref, xp_ref, xn_ref, mod_ref, g_ref = (next(it) for _ in range(6))
    wa_ref, wb_ref, cwa_ref, cwb_ref, cba_ref, cbb_ref, wd_ref = (next(it) for _ in range(7))
    if final:
        gf_ref = next(it)
    o_ref, h_ref, act_ref = next(it), next(it), next(it)
    u_refs = list(it)

    f = pl.program_id(1)
    n = pl.program_id(0) % tiles_per_seq
    d = x_ref.shape[1]
    tf = wa_ref.shape[1]
    sub = FFN_SUB_COLS
    n_sub = tf // sub
    prev_halo, next_halo = tm, tm + FFN_HALO
    row0 = zero_ref[0]

    @pl.when(f == 0)
    def _():
        g, sc, sh = g_ref[...], mod_ref[0, 4:5, :], mod_ref[0, 3:4, :]
        gs = g * (1.0 + sc)

        def norm_rows(r, carry):
            rs = pl.ds(pl.multiple_of(r * NORM_ROWS, NORM_ROWS), NORM_ROWS)
            x = x_ref[rs, :]
            ms = jnp.mean(x * x, axis=-1, keepdims=True)
            h_ref[rs, :] = (x * lax.rsqrt(ms + EPS) * gs + sh).astype(BF16)
            return carry

        lax.fori_loop(0, tm // NORM_ROWS, norm_rows, 0)
        hp = jnp.where(n > 0, _rms_mod(xp_ref[...], g, sc, sh), 0.0)
        hn = jnp.where(n < tiles_per_seq - 1, _rms_mod(xn_ref[...], g, sc, sh), 0.0)
        h_ref[prev_halo:prev_halo + FFN_HALO, :] = hp.astype(BF16)
        h_ref[next_halo:next_halo + FFN_HALO, :] = hn.astype(BF16)
        o_ref[...] = jnp.zeros((tm, d), F32)

    def up(s):
        h = h_ref[...]
        for j, w_ref in enumerate((wa_ref, wb_ref)):
            u_refs[s][:, j * sub:(j + 1) * sub] = jnp.dot(h, w_ref[:, s * sub:(s + 1) * sub],
                                                          preferred_element_type=F32)

    def gate_rows(s, r0, r1):
        rows = r1 - r0
        zeros = jnp.zeros((FFN_HALO, sub), F32)
        mid = slice(FFN_HALO, FFN_HALO + rows)
        row = lax.broadcasted_iota(jnp.int32, (rows, 1), 0) + r0
        inner = [b for b in range(seg, tm, seg) if r0 < b < r1]
        hid = slice(s * sub, (s + 1) * sub)

        def u_rows(start, nrows, cols):
            return u_refs[s][pl.ds(pl.multiple_of(row0 + start, V7X_SUBLANES), nrows), cols]

        def conv(j, cw_ref, cb_ref):
            cols = slice(j * sub, (j + 1) * sub)
            if r0 == 0:
                top = u_rows(prev_halo, FFN_HALO, cols)
            elif r0 % seg == 0:
                top = zeros
            else:
                top = u_rows(r0 - FFN_HALO, FFN_HALO, cols)
            if r1 == tm:
                bot = u_rows(next_halo, FFN_HALO, cols)
            elif r1 % seg == 0:
                bot = zeros
            else:
                bot = u_rows(r1, FFN_HALO, cols)
            ext = jnp.concatenate([top, u_rows(r0, rows, cols), bot], axis=0)
            prev = pltpu.roll(ext, 1, 0)[mid]
            nxt = pltpu.roll(ext, rows + 2 * FFN_HALO - 1, 0)[mid]
            for b in inner:
                prev = jnp.where(row == b, 0.0, prev)
                nxt = jnp.where(row == b - 1, 0.0, nxt)
            return cw_ref[0:1, hid] * prev + cw_ref[1:2, hid] * ext[mid] + cw_ref[2:3, hid] * nxt + cb_ref[:, hid]

        a = conv(0, cwa_ref, cba_ref)
        b = conv(1, cwb_ref, cbb_ref)
        act_ref[r0:r1, hid] = (a * _sigmoid(a) * b).astype(BF16)

    chunk = tm // FFN_GATE_CHUNKS
    for s in range(n_sub):
        up(s)
    for s in range(n_sub):
        for k in range(FFN_GATE_CHUNKS):
            gate_rows(s, k * chunk, (k + 1) * chunk)
        o_ref[...] += jnp.dot(act_ref[:, s * sub:(s + 1) * sub], wd_ref[s * sub:(s + 1) * sub, :],
                              preferred_element_type=F32)

    @pl.when(f == nf - 1)
    def _():
        y = x_ref[...] + mod_ref[0, 5:6, :] * o_ref[...]
        if final:
            ms = jnp.mean(y * y, axis=-1, keepdims=True)
            y = y * lax.rsqrt(ms + EPS) * gf_ref[...]
        o_ref[...] = y


def _ffn(x, mod, g_norm, w_up, cw, cb, w_down, g_final, *, tm, tf, seg, tiles_per_seq, tiles_per_mod):
    n, d = x.shape
    d_ff = w_down.shape[0]
    nf = d_ff // tf
    hb = tm // FFN_HALO
    nhb = n // FFN_HALO
    final = g_final is not None
    in_specs = [
        pl.BlockSpec(memory_space=pltpu.SMEM),
        pl.BlockSpec((tm, d), lambda i, f: (i, 0), pipeline_mode=pl.Buffered(1)),
        pl.BlockSpec((FFN_HALO, d), lambda i, f: (jnp.maximum(i * hb - 1, 0), 0)),
        pl.BlockSpec((FFN_HALO, d), lambda i, f: (jnp.minimum((i + 1) * hb, nhb - 1), 0)),
        pl.BlockSpec((1, 6, d), lambda i, f: (i // tiles_per_mod, 0, 0)),
        _const_spec((1, d)),
        pl.BlockSpec((d, tf), lambda i, f: (0, f)),
        pl.BlockSpec((d, tf), lambda i, f: (0, nf + f)),
        pl.BlockSpec((FFN_K, tf), lambda i, f: (0, f)),
        pl.BlockSpec((FFN_K, tf), lambda i, f: (0, nf + f)),
        pl.BlockSpec((1, tf), lambda i, f: (0, f)),
        pl.BlockSpec((1, tf), lambda i, f: (0, nf + f)),
        pl.BlockSpec((tf, d), lambda i, f: (f, 0)),
    ]
    args = [jnp.zeros((1,), jnp.int32), x, x, x, mod, g_norm, w_up, w_up, cw, cw, cb, cb, w_down]
    if final:
        in_specs.append(_const_spec((1, d)))
        args.append(g_final)
    return pl.pallas_call(
        functools.partial(_ffn_kernel, tm=tm, seg=seg, tiles_per_seq=tiles_per_seq, nf=nf, final=final),
        grid=(n // tm, nf),
        in_specs=in_specs,
        out_specs=pl.BlockSpec((tm, d), lambda i, f: (i, 0)),
        out_shape=jax.ShapeDtypeStruct((n, d), F32),
        scratch_shapes=[pltpu.VMEM((tm + 2 * FFN_HALO, d), BF16), pltpu.VMEM((tm, tf), BF16)]
        + [pltpu.VMEM((tm + 2 * FFN_HALO, 2 * FFN_SUB_COLS), F32)] * (tf // FFN_SUB_COLS),
        compiler_params=_params(("arbitrary", "arbitrary")),
        name="ffn",
    )(*args)


def _rope_tables(seq_len):
    rows = seq_len // GRID_W
    row = jnp.repeat(jnp.arange(rows, dtype=F32), GRID_W)
    col = jnp.tile(jnp.arange(GRID_W, dtype=F32), rows)
    n_freq = HEAD_DIM // 4
    inv = ROPE_THETA ** (-jnp.arange(n_freq, dtype=F32) / n_freq)
    ang = jnp.concatenate([row[:, None] * inv, col[:, None] * inv], axis=-1)
    cos, sin = jnp.cos(ang), jnp.sin(ang)
    reps = V7X_LANES // HEAD_DIM
    return jnp.tile(jnp.concatenate([cos, cos], -1), (1, reps)), jnp.tile(jnp.concatenate([-sin, sin], -1), (1, reps))


def _gmajor_cols(w, d_in):
    return w.reshape(d_in, N_KV_HEADS, Q_PER_KV, HEAD_DIM).transpose(0, 2, 1, 3).reshape(d_in, -1)


def kernel(x_prompt, x_sample, cache_k, cache_v, c, c_ctx, w_ada, b_ada, g_norm1, w_in, attn_sink, conv_dw_w, conv_dw_b, conv_ln_g, conv_ln_b, gmlp_ln_g, gmlp_ln_b, gmlp_ws, gmlp_bs, w_out, g_norm2, w_up, ffn_dw_w, ffn_dw_b, w_down, g_final):
    b1, l1, d = x_prompt.shape
    b2, l2, _ = x_sample.shape
    depth = w_ada.shape[0]
    past = cache_k.shape[2]
    kv_dim = N_KV_HEADS * HEAD_DIM
    attn_dim = N_KV_HEADS * Q_PER_KV * HEAD_DIM
    conv_dim = conv_dw_w.shape[-1]
    gmlp_dim = gmlp_ln_g.shape[1] * gmlp_ln_g.shape[2]
    dims = (attn_dim, kv_dim, conv_dim, gmlp_dim)

    n_cond = 1 + b2
    cond = jnp.concatenate([c_ctx[None, :], c], axis=0)
    cond = jnp.pad(cond, ((0, (-n_cond) % V7X_SUBLANES), (0, 0)))
    mod_all = _ada(cond, w_ada, b_ada)

    rope = _rope_tables(l2)
    xp = x_prompt.reshape(b1 * l1, d)
    xs = x_sample.reshape(b2 * l2, d)
    new_k, new_v = [], []

    for l in range(depth):
        w_in_l = jnp.concatenate([_gmajor_cols(w_in[l, :, :attn_dim], d), w_in[l, :, attn_dim:]], axis=1).astype(BF16)
        w_out_l = jnp.concatenate([_gmajor_cols(w_out[l, :attn_dim].T, d).T, w_out[l, attn_dim:]], axis=0).astype(BF16)
        w_up_l = w_up[l].astype(BF16)
        w_down_l = w_down[l].astype(BF16)
        gmlp = (gmlp_ln_g[l].reshape(1, gmlp_dim), gmlp_ln_b[l].reshape(1, gmlp_dim), gmlp_ws[l].astype(BF16),
                jnp.repeat(gmlp_bs[l].T, GMLP_GROUP_DIM, axis=1))
        conv = (jnp.broadcast_to(conv_dw_w[l][:, None, :], (CONV_K, V7X_SUBLANES, conv_dim)), conv_dw_b[l][None, :],
                conv_ln_g[l][None, :], conv_ln_b[l][None, :])
        mod = mod_all[l].reshape(-1, 6, d)
        g_last = g_final[None, :] if l == depth - 1 else None

        def layer(x, seq_len, mod_rows, per_seq_mod, rope_l, window_kv, ctx_kv, emit_kv):
            def tiling(tm):
                tps = max(seq_len // tm, 1)
                return min(tm, seq_len), tps, tps if per_seq_mod else x.shape[0] // tm

            tm = MIX_TILE_ROWS
            seg, tps, tpm = tiling(tm)
            outs = _inproj(x, mod_rows, g_norm1[l][None, :], w_in_l, rope_l, gmlp, tm=tm, tiles_per_mod=tpm,
                           dims=dims, emit_kv_f32=emit_kv)
            q, k, v, glu, gm = outs[:5]
            if window_kv:
                attn = _attention(q, k, v, ctx_kv[0], ctx_kv[1], attn_sink[l], nb=seq_len // BLOCK, local=True)
            else:
                nseq = x.shape[0] // seq_len
                attn = _attention(q, None, None, k.reshape(nseq, seq_len, kv_dim), v.reshape(nseq, seq_len, kv_dim),
                                  attn_sink[l], nb=seq_len // BLOCK, local=False)
            x = _outproj(attn, glu, gm, x, mod_rows, conv, w_out_l, tm=tm, seg=seg, tiles_per_seq=tps,
                         tiles_per_mod=tpm)
            tmf = FFN_TILE_ROWS
            segf, tpsf, tpmf = tiling(tmf)
            x = _ffn(x, mod_rows, g_norm2[l][None, :], w_up_l, ffn_dw_w[l], ffn_dw_b[l][None, :], w_down_l, g_last,
                     tm=tmf, tf=FFN_TILE_COLS, seg=segf, tiles_per_seq=tpsf, tiles_per_mod=tpmf)
            return x, outs[5:]

        xp, kv_f32 = layer(xp, l1, mod[0:1], False, None, False, None, True)
        new_k.append(kv_f32[0].reshape(b1, l1, N_KV_HEADS, HEAD_DIM))
        new_v.append(kv_f32[1].reshape(b1, l1, N_KV_HEADS, HEAD_DIM))
        ctx_kv = (cache_k[:, l].reshape(b2, past, kv_dim).astype(BF16), cache_v[:, l].reshape(b2, past, kv_dim).astype(BF16))
        xs, _ = layer(xs, l2, mod[1:1 + b2], True, rope, True, ctx_kv, False)

    return (xp.reshape(b1, l1, d), xs.reshape(b2, l2, d), jnp.stack(new_k, axis=1), jnp.stack(new_v, axis=1))
```

```python
import functools

import jax
import jax.numpy as jnp
from jax import lax
from jax.experimental import pallas as pl
from jax.experimental.pallas import tpu as pltpu

F32 = jnp.float32
BF16 = jnp.bfloat16

HEAD_DIM = 64
N_KV_HEADS = 4
Q_PER_KV = 4
GRID_W = 64
BLOCK = 128
CHUNK = 128
GMLP_GROUP_DIM = 128
CONV_K = 31
FFN_K = 3
ROPE_THETA = 10000.0
EPS = 1e-6
NEG = -1e30
SCALE = HEAD_DIM ** -0.5

V7X_LANES = 128
V7X_SUBLANES = 8
V7X_VMEM_LIMIT_BYTES = 56 * 1024 * 1024

MIX_TILE_ROWS = 512
FFN_TILE_ROWS = 1024
FFN_TILE_COLS = 512
FFN_SUB_COLS = 256
FFN_ROW_PARTS = 2
FFN_GATE_CHUNKS = 4
OUT_COLS = 256

CONV_HALO = 16
CONV_ROWS = 32
FFN_HALO = V7X_SUBLANES
NORM_ROWS = 64


def _sigmoid(x):
    return 1.0 / (1.0 + jnp.exp(-x))


def _rms_mod(x, g, sc, sh):
    ms = jnp.mean(x * x, axis=-1, keepdims=True)
    return (x * lax.rsqrt(ms + EPS) * g) * (1.0 + sc) + sh


def _const_spec(shape):
    nd = len(shape)
    return pl.BlockSpec(shape, lambda *_: (0,) * nd, pipeline_mode=pl.Buffered(1))


def _params(sem):
    return pltpu.CompilerParams(dimension_semantics=sem, vmem_limit_bytes=V7X_VMEM_LIMIT_BYTES)


def _ada_kernel(c_ref, w_ref, b_ref, o_ref):
    c = c_ref[...]
    s = (c * _sigmoid(c)).astype(BF16)
    o_ref[0] = jnp.dot(s, w_ref[0].astype(BF16), preferred_element_type=F32) + b_ref[0]


def _ada(cond, w_ada, b_ada):
    depth, d, n = w_ada.shape
    r = cond.shape[0]
    tn = 1024
    return pl.pallas_call(
        _ada_kernel,
        grid=(depth, n // tn),
        in_specs=[
            pl.BlockSpec((r, d), lambda l, j: (0, 0)),
            pl.BlockSpec((1, d, tn), lambda l, j: (l, 0, j)),
            pl.BlockSpec((1, 1, tn), lambda l, j: (l, 0, j)),
        ],
        out_specs=pl.BlockSpec((1, r, tn), lambda l, j: (l, 0, j)),
        out_shape=jax.ShapeDtypeStruct((depth, r, n), F32),
        compiler_params=_params(("arbitrary", "arbitrary")),
        name="ada",
    )(cond, w_ada, b_ada.reshape(depth, 1, n))


def _rope(t, cos, sin):
    outs = []
    lane = lax.broadcasted_iota(jnp.int32, (t.shape[0], V7X_LANES), 1)
    first = (lane % HEAD_DIM) < (HEAD_DIM // 2)
    for j in range(t.shape[1] // V7X_LANES):
        tj = t[:, j * V7X_LANES:(j + 1) * V7X_LANES]
        rot = jnp.where(first, pltpu.roll(tj, V7X_LANES - HEAD_DIM // 2, 1), pltpu.roll(tj, HEAD_DIM // 2, 1))
        outs.append(tj * cos + rot * sin)
    return jnp.concatenate(outs, axis=1)


def _inproj_kernel(*refs, tm, dims, use_rope, emit_kv_f32):
    attn_dim, kv_dim, conv_dim, gmlp_dim = dims
    it = iter(refs)
    x_ref, mod_ref, g_ref, w_ref = next(it), next(it), next(it), next(it)
    if use_rope:
        cos_ref, sin_ref = next(it), next(it)
    lng_ref, lnb_ref, ws_ref, bs_ref = next(it), next(it), next(it), next(it)
    q_ref, k_ref, v_ref, glu_ref, gm_ref = next(it), next(it), next(it), next(it), next(it)
    if emit_kv_f32:
        kf_ref, vf_ref = next(it), next(it)

    h = _rms_mod(x_ref[...], g_ref[...], mod_ref[0, 1:2, :], mod_ref[0, 0:1, :]).astype(BF16)

    def proj(lo, n):
        return jnp.dot(h, w_ref[:, lo:lo + n], preferred_element_type=F32)

    o_k = attn_dim
    o_v = o_k + kv_dim
    o_a = o_v + kv_dim
    o_g = o_a + conv_dim
    o_u = o_g + conv_dim
    o_w = o_u + gmlp_dim

    q = proj(0, attn_dim)
    k = proj(o_k, kv_dim)
    v = proj(o_v, kv_dim)
    if emit_kv_f32:
        kf_ref[...] = k
        vf_ref[...] = v
    if use_rope:
        cos, sin = cos_ref[...], sin_ref[...]
        q = _rope(q, cos, sin)
        k = _rope(k, cos, sin)
    q_ref[...] = q.astype(BF16)
    k_ref[...] = k.astype(BF16)
    v_ref[...] = v.astype(BF16)

    glu_ref[...] = proj(o_a, conv_dim) * _sigmoid(proj(o_g, conv_dim))

    u = jax.nn.gelu(proj(o_u, gmlp_dim))
    w = jax.nn.gelu(proj(o_w, gmlp_dim))
    for g in range(gmlp_dim // GMLP_GROUP_DIM):
        cs = slice(g * GMLP_GROUP_DIM, (g + 1) * GMLP_GROUP_DIM)
        wg = w[:, cs]
        mu = jnp.mean(wg, axis=-1, keepdims=True)
        var = jnp.mean(jnp.square(wg - mu), axis=-1, keepdims=True)
        wn = ((wg - mu) * lax.rsqrt(var + EPS) * lng_ref[:, cs] + lnb_ref[:, cs]).astype(BF16)
        for c in range(tm // CHUNK):
            rs = slice(c * CHUNK, (c + 1) * CHUNK)
            sv = jnp.dot(ws_ref[g], wn[rs, :], preferred_element_type=F32) + bs_ref[:, cs]
            gm_ref[rs, cs] = (u[rs, cs] * sv).astype(BF16)


def _inproj(x, mod, g_norm, w_in, rope, gmlp, *, tm, tiles_per_mod, dims, emit_kv_f32):
    n, d = x.shape
    attn_dim, kv_dim, conv_dim, gmlp_dim = dims
    lng, lnb, ws, bs = gmlp
    row = lambda w: pl.BlockSpec((tm, w), lambda i: (i, 0))
    in_specs = [
        row(d),
        pl.BlockSpec((1, 6, d), lambda i: (i // tiles_per_mod, 0, 0)),
        _const_spec((1, d)),
        _const_spec(w_in.shape),
    ]
    args = [x, mod, g_norm, w_in]
    if rope is not None:
        tiles_per_seq = rope[0].shape[0] // tm
        in_specs += [pl.BlockSpec((tm, V7X_LANES), lambda i: (i % tiles_per_seq, 0))] * 2
        args += list(rope)
    in_specs += [_const_spec(lng.shape), _const_spec(lnb.shape), _const_spec(ws.shape), _const_spec(bs.shape)]
    args += [lng, lnb, ws, bs]
    out_specs = [row(attn_dim), row(kv_dim), row(kv_dim), row(conv_dim), row(gmlp_dim)]
    out_shape = [
        jax.ShapeDtypeStruct((n, attn_dim), BF16),
        jax.ShapeDtypeStruct((n, kv_dim), BF16),
        jax.ShapeDtypeStruct((n, kv_dim), BF16),
        jax.ShapeDtypeStruct((n, conv_dim), F32),
        jax.ShapeDtypeStruct((n, gmlp_dim), BF16),
    ]
    if emit_kv_f32:
        out_specs += [row(kv_dim), row(kv_dim)]
        out_shape += [jax.ShapeDtypeStruct((n, kv_dim), F32)] * 2
    return pl.pallas_call(
        functools.partial(_inproj_kernel, tm=tm, dims=dims, use_rope=rope is not None, emit_kv_f32=emit_kv_f32),
        grid=(n // tm,),
        in_specs=in_specs,
        out_specs=out_specs,
        out_shape=out_shape,
        compiler_params=_params(("arbitrary",)),
        name="inproj",
    )(*args)


def _attn_kernel(*refs, nb, local):
    it = iter(refs)
    sink_ref, q_ref = next(it), next(it)
    if local:
        kl_refs = (next(it), next(it), next(it))
        vl_refs = (next(it), next(it), next(it))
    ck_ref, cv_ref, o_ref = next(it), next(it), next(it)

    rows = Q_PER_KV * BLOCK
    lane = lax.broadcasted_iota(jnp.int32, (rows, V7X_LANES), 1)
    rowg = lax.broadcasted_iota(jnp.int32, (rows, 1), 0) // BLOCK
    upper = lane >= HEAD_DIM
    nt = (((1,), (1,)), ((), ()))

    if local:
        n = pl.program_id(0) % nb
        qi = lax.broadcasted_iota(jnp.int32, (rows, 3 * BLOCK), 0) % BLOCK
        kj = lax.broadcasted_iota(jnp.int32, (rows, 3 * BLOCK), 1)
        lo = jnp.where(n == 0, BLOCK, 0)
        hi = jnp.where(n == nb - 1, 2 * BLOCK, 3 * BLOCK)
        valid = (kj >= qi) & (kj <= qi + 2 * BLOCK) & (kj >= lo) & (kj < hi)

    for p in range(N_KV_HEADS // 2):
        ls = slice(p * V7X_LANES, (p + 1) * V7X_LANES)
        qs = jnp.concatenate(
            [q_ref[:, g * 2 * V7X_LANES + p * V7X_LANES: g * 2 * V7X_LANES + (p + 1) * V7X_LANES]
             for g in range(Q_PER_KV)], axis=0) * SCALE
        ck = ck_ref[0, :, ls]
        cv = cv_ref[0, :, ls]
        if local:
            kl = jnp.concatenate([r[:, ls] for r in kl_refs], axis=0)
            vl = jnp.concatenate([r[:, ls] for r in vl_refs], axis=0)
        outs = []
        for hh in range(2):
            kvh = 2 * p + hh
            qm = jnp.where(upper if hh else jnp.logical_not(upper), qs, jnp.zeros_like(qs))
            sk = jnp.zeros((rows, 1), F32)
            for g in range(Q_PER_KV):
                sk = jnp.where(rowg == g, sink_ref[kvh * Q_PER_KV + g], sk)
            s_ctx = lax.dot_general(qm, ck, nt, preferred_element_type=F32)
            m = jnp.maximum(jnp.max(s_ctx, axis=-1, keepdims=True), sk)
            if local:
                s_loc = lax.dot_general(qm, kl, nt, preferred_element_type=F32)
                s_loc = jnp.where(valid, s_loc, NEG)
                m = jnp.maximum(m, jnp.max(s_loc, axis=-1, keepdims=True))
            p_ctx = jnp.exp(s_ctx - m)
            den = jnp.sum(p_ctx, axis=-1, keepdims=True) + jnp.exp(sk - m)
            acc = jnp.dot(p_ctx.astype(BF16), cv, preferred_element_type=F32)
            if local:
                p_loc = jnp.exp(s_loc - m)
                den = den + jnp.sum(p_loc, axis=-1, keepdims=True)
                acc = acc + jnp.dot(p_loc.astype(BF16), vl, preferred_element_type=F32)
            outs.append(acc / den)
        o = jnp.where(upper, outs[1], outs[0]).astype(BF16)
        for g in range(Q_PER_KV):
            o_ref[:, g * 2 * V7X_LANES + p * V7X_LANES: g * 2 * V7X_LANES + (p + 1) * V7X_LANES] = (
                o[g * BLOCK:(g + 1) * BLOCK, :])


def _attention(q, k, v, ck, cv, sink, *, nb, local):
    n, attn_dim = q.shape
    kv_dim = ck.shape[-1]
    nblk = n // BLOCK
    in_specs = [pl.BlockSpec(memory_space=pltpu.SMEM), pl.BlockSpec((BLOCK, attn_dim), lambda i: (i, 0))]
    args = [sink, q]
    if local:
        maps = [lambda i: (jnp.maximum(i - 1, 0), 0), lambda i: (i, 0), lambda i: (jnp.minimum(i + 1, nblk - 1), 0)]
        in_specs += [pl.BlockSpec((BLOCK, kv_dim), m) for m in maps] * 2
        args += [k, k, k, v, v, v]
    in_specs += [pl.BlockSpec((1,) + ck.shape[1:], lambda i: (i // nb, 0, 0))] * 2
    args += [ck, cv]
    return pl.pallas_call(
        functools.partial(_attn_kernel, nb=nb, local=local),
        grid=(nblk,),
        in_specs=in_specs,
        out_specs=pl.BlockSpec((BLOCK, attn_dim), lambda i: (i, 0)),
        out_shape=jax.ShapeDtypeStruct((n, attn_dim), BF16),
        compiler_params=_params(("arbitrary",)),
        name="attention",
    )(*args)


def _outproj_kernel(attn_ref, glu_ref, glup_ref, glun_ref, gm_ref, x_ref, mod_ref, cw_ref, cb_ref,
                    clg_ref, clb_ref, wo_ref, o_ref, pad_ref, shift_ref, conv_ref, *, tm, seg, tiles_per_seq):
    n = pl.program_id(0) % tiles_per_seq
    attn_dim = attn_ref.shape[1]
    conv_dim = glu_ref.shape[1]
    stride = seg + 2 * CONV_HALO
    total = (tm // seg) * stride
    zeros = jnp.zeros((CONV_HALO, conv_dim), F32)

    for s in range(tm // seg):
        base = s * stride
        lead, trail = zeros, zeros
        if tiles_per_seq > 1:
            lead = jnp.where(n > 0, glup_ref[...], 0.0)
            trail = jnp.where(n < tiles_per_seq - 1, glun_ref[...], 0.0)
        pad_ref[base:base + CONV_HALO, :] = lead
        pad_ref[base + CONV_HALO:base + CONV_HALO + seg, :] = glu_ref[s * seg:(s + 1) * seg, :]
        pad_ref[base + CONV_HALO + seg:base + stride, :] = trail
    for b in range(1, V7X_SUBLANES):
        shift_ref[b - 1, 0:total - V7X_SUBLANES, :] = pad_ref[b:b + total - V7X_SUBLANES, :]

    first_tap = CONV_HALO - (CONV_K - 1) // 2
    groups = CONV_ROWS // V7X_SUBLANES

    def conv_block(s, r0):
        accs = [jnp.zeros((V7X_SUBLANES, conv_dim), F32) + cb_ref[...]] * groups
        for j in range(CONV_K):
            off = first_tap + j
            lo = s * stride + r0 + (off // V7X_SUBLANES) * V7X_SUBLANES
            sub = off % V7X_SUBLANES
            w = cw_ref[j]
            for q in range(groups):
                rs = slice(lo + q * V7X_SUBLANES, lo + (q + 1) * V7X_SUBLANES)
                win = pad_ref[rs, :] if sub == 0 else shift_ref[sub - 1, rs, :]
                accs[q] = accs[q] + win * w
        acc = jnp.concatenate(accs, axis=0)
        mu = jnp.mean(acc, axis=-1, keepdims=True)
        var = jnp.mean(jnp.square(acc - mu), axis=-1, keepdims=True)
        y = (acc - mu) * lax.rsqrt(var + EPS) * clg_ref[...] + clb_ref[...]
        conv_ref[s * seg + r0:s * seg + r0 + CONV_ROWS, :] = (y * _sigmoid(y)).astype(BF16)

    blocks = [(s, r0) for s in range(tm // seg) for r0 in range(0, seg, CONV_ROWS)]
    n_col_tiles = o_ref.shape[1] // OUT_COLS
    per_tile = -(-len(blocks) // n_col_tiles)
    for k in range(n_col_tiles):
        cols = slice(k * OUT_COLS, (k + 1) * OUT_COLS)
        part = jnp.dot(attn_ref[...], wo_ref[0:attn_dim, cols], preferred_element_type=F32)
        o_ref[:, cols] = part + jnp.dot(gm_ref[...], wo_ref[attn_dim + conv_dim:, cols], preferred_element_type=F32)
        for blk in blocks[k * per_tile:(k + 1) * per_tile]:
            conv_block(*blk)
    g1 = mod_ref[0, 2:3, :]
    for k in range(n_col_tiles):
        cols = slice(k * OUT_COLS, (k + 1) * OUT_COLS)
        part = jnp.dot(conv_ref[...], wo_ref[attn_dim:attn_dim + conv_dim, cols], preferred_element_type=F32)
        o_ref[:, cols] = x_ref[:, cols] + g1[:, cols] * (o_ref[:, cols] + part)


def _outproj(attn, glu, gm, x, mod, conv, w_out, *, tm, seg, tiles_per_seq, tiles_per_mod):
    n, d = x.shape
    cw, cb, clg, clb = conv
    conv_dim = glu.shape[1]
    hb = tm // CONV_HALO
    nhb = n // CONV_HALO
    pad_rows = (tm // seg) * (seg + 2 * CONV_HALO)
    row = lambda w: pl.BlockSpec((tm, w), lambda i: (i, 0))
    return pl.pallas_call(
        functools.partial(_outproj_kernel, tm=tm, seg=seg, tiles_per_seq=tiles_per_seq),
        grid=(n // tm,),
        in_specs=[
            row(attn.shape[1]),
            row(conv_dim),
            pl.BlockSpec((CONV_HALO, conv_dim), lambda i: (jnp.maximum(i * hb - 1, 0), 0)),
            pl.BlockSpec((CONV_HALO, conv_dim), lambda i: (jnp.minimum((i + 1) * hb, nhb - 1), 0)),
            row(gm.shape[1]),
            row(d),
            pl.BlockSpec((1, 6, d), lambda i: (i // tiles_per_mod, 0, 0)),
            _const_spec(cw.shape), _const_spec(cb.shape), _const_spec(clg.shape), _const_spec(clb.shape),
            _const_spec(w_out.shape),
        ],
        out_specs=row(d),
        out_shape=jax.ShapeDtypeStruct((n, d), F32),
        scratch_shapes=[
            pltpu.VMEM((pad_rows, conv_dim), F32),
            pltpu.VMEM((V7X_SUBLANES - 1, pad_rows, conv_dim), F32),
            pltpu.VMEM((tm, conv_dim), BF16),
        ],
        compiler_params=_params(("arbitrary",)),
        name="outproj",
    )(attn, glu, glu, glu, gm, x, mod, cw, cb, clg, clb, w_out)


def _ffn_kernel(*refs, tm, seg, tiles_per_seq, nf, final):
    it = iter(refs)
    zero_ref, x_ref, xp_ref, xn_ref, mod_ref, g_ref = (next(it) for _ in range(6))
    wa_ref, wb_ref, cwa_ref, cwb_ref, cba_ref, cbb_ref, wd_ref = (next(it) for _ in range(7))
    if final:
        gf_ref = next(it)
    o_ref, act_ref = next(it), next(it)
    tf = wa_ref.shape[1]
    sub = FFN_SUB_COLS
    n_sub = tf // sub
    parts = FFN_ROW_PARTS
    h_refs = [next(it) for _ in range(parts)]
    u_refs = [[next(it) for _ in range(parts)] for _ in range(n_sub)]

    f = pl.program_id(1)
    n = pl.program_id(0) % tiles_per_seq
    d = x_ref.shape[1]
    pr = tm // parts
    prev_halo, next_halo = pr, pr + FFN_HALO
    row0 = zero_ref[0]

    @pl.when(f == 0)
    def _():
        g, sc, sh = g_ref[...], mod_ref[0, 4:5, :], mod_ref[0, 3:4, :]
        gs = g * (1.0 + sc)

        def norm(x):
            ms = jnp.mean(x * x, axis=-1, keepdims=True)
            return x * lax.rsqrt(ms + EPS) * gs + sh

        for p in range(parts):
            def norm_rows(r, carry, p=p):
                lo = pl.multiple_of(r * NORM_ROWS, NORM_ROWS)
                h_refs[p][pl.ds(lo, NORM_ROWS), :] = norm(x_ref[pl.ds(p * pr + lo, NORM_ROWS), :]).astype(BF16)
                return carry

            lax.fori_loop(0, pr // NORM_ROWS, norm_rows, 0)
            lo, hi = p * pr, (p + 1) * pr
            if p == 0:
                before = jnp.where(n > 0, norm(xp_ref[...]), 0.0)
            elif lo % seg == 0:
                before = jnp.zeros((FFN_HALO, d), F32)
            else:
                before = norm(x_ref[lo - FFN_HALO:lo, :])
            if p == parts - 1:
                after = jnp.where(n < tiles_per_seq - 1, norm(xn_ref[...]), 0.0)
            elif hi % seg == 0:
                after = jnp.zeros((FFN_HALO, d), F32)
            else:
                after = norm(x_ref[hi:hi + FFN_HALO, :])
            h_refs[p][prev_halo:prev_halo + FFN_HALO, :] = before.astype(BF16)
            h_refs[p][next_halo:next_halo + FFN_HALO, :] = after.astype(BF16)
        o_ref[...] = jnp.zeros((tm, d), F32)

    def up(s, p):
        h = h_refs[p][...]
        for j, w_ref in enumerate((wa_ref, wb_ref)):
            u_refs[s][p][:, j * sub:(j + 1) * sub] = jnp.dot(h, w_ref[:, s * sub:(s + 1) * sub],
                                                             preferred_element_type=F32)

    def gate_rows(s, p, r0, r1):
        rows = r1 - r0
        base = p * pr
        zeros = jnp.zeros((FFN_HALO, sub), F32)
        mid = slice(FFN_HALO, FFN_HALO + rows)
        row = lax.broadcasted_iota(jnp.int32, (rows, 1), 0) + (base + r0)
        inner = [b for b in range(seg, tm, seg) if base + r0 < b < base + r1]
        hid = slice(s * sub, (s + 1) * sub)

        def u_rows(start, nrows, cols):
            return u_refs[s][p][pl.ds(pl.multiple_of(row0 + start, V7X_SUBLANES), nrows), cols]

        def conv(j, cw_ref, cb_ref):
            cols = slice(j * sub, (j + 1) * sub)
            if r0 == 0:
                top = u_rows(prev_halo, FFN_HALO, cols)
            elif (base + r0) % seg == 0:
                top = zeros
            else:
                top = u_rows(r0 - FFN_HALO, FFN_HALO, cols)
            if r1 == pr:
                bot = u_rows(next_halo, FFN_HALO, cols)
            elif (base + r1) % seg == 0:
                bot = zeros
            else:
                bot = u_rows(r1, FFN_HALO, cols)
            ext = jnp.concatenate([top, u_rows(r0, rows, cols), bot], axis=0)
            prev = pltpu.roll(ext, 1, 0)[mid]
            nxt = pltpu.roll(ext, rows + 2 * FFN_HALO - 1, 0)[mid]
            for b in inner:
                prev = jnp.where(row == b, 0.0, prev)
                nxt = jnp.where(row == b - 1, 0.0, nxt)
            return cw_ref[0:1, hid] * prev + cw_ref[1:2, hid] * ext[mid] + cw_ref[2:3, hid] * nxt + cb_ref[:, hid]

        a = conv(0, cwa_ref, cba_ref)
        b = conv(1, cwb_ref, cbb_ref)
        act_ref[base + r0:base + r1, hid] = (a * _sigmoid(a) * b).astype(BF16)

    chunk = pr // FFN_GATE_CHUNKS
    for s in range(n_sub):
        for p in range(parts):
            up(s, p)
    for s in range(n_sub):
        for p in range(parts):
            for k in range(FFN_GATE_CHUNKS):
                gate_rows(s, p, k * chunk, (k + 1) * chunk)
            rs = slice(p * pr, (p + 1) * pr)
            o_ref[rs, :] += jnp.dot(act_ref[rs, s * sub:(s + 1) * sub], wd_ref[s * sub:(s + 1) * sub, :],
                                    preferred_element_type=F32)

    @pl.when(f == nf - 1)
    def _():
        y = x_ref[...] + mod_ref[0, 5:6, :] * o_ref[...]
        if final:
            ms = jnp.mean(y * y, axis=-1, keepdims=True)
            y = y * lax.rsqrt(ms + EPS) * gf_ref[...]
        o_ref[...] = y


def _ffn(x, mod, g_norm, w_up, cw, cb, w_down, g_final, *, tm, tf, seg, tiles_per_seq, tiles_per_mod):
    n, d = x.shape
    d_ff = w_down.shape[0]
    nf = d_ff // tf
    hb = tm // FFN_HALO
    nhb = n // FFN_HALO
    final = g_final is not None
    part_rows = tm // FFN_ROW_PARTS + 2 * FFN_HALO
    in_specs = [
        pl.BlockSpec(memory_space=pltpu.SMEM),
        pl.BlockSpec((tm, d), lambda i, f: (i, 0), pipeline_mode=pl.Buffered(1)),
        pl.BlockSpec((FFN_HALO, d), lambda i, f: (jnp.maximum(i * hb - 1, 0), 0)),
        pl.BlockSpec((FFN_HALO, d), lambda i, f: (jnp.minimum((i + 1) * hb, nhb - 1), 0)),
        pl.BlockSpec((1, 6, d), lambda i, f: (i // tiles_per_mod, 0, 0)),
        _const_spec((1, d)),
        pl.BlockSpec((d, tf), lambda i, f: (0, f)),
        pl.BlockSpec((d, tf), lambda i, f: (0, nf + f)),
        pl.BlockSpec((FFN_K, tf), lambda i, f: (0, f)),
        pl.BlockSpec((FFN_K, tf), lambda i, f: (0, nf + f)),
        pl.BlockSpec((1, tf), lambda i, f: (0, f)),
        pl.BlockSpec((1, tf), lambda i, f: (0, nf + f)),
        pl.BlockSpec((tf, d), lambda i, f: (f, 0)),
    ]
    args = [jnp.zeros((1,), jnp.int32), x, x, x, mod, g_norm, w_up, w_up, cw, cw, cb, cb, w_down]
    if final:
        in_specs.append(_const_spec((1, d)))
        args.append(g_final)
    return pl.pallas_call(
        functools.partial(_ffn_kernel, tm=tm, seg=seg, tiles_per_seq=tiles_per_seq, nf=nf, final=final),
        grid=(n // tm, nf),
        in_specs=in_specs,
        out_specs=pl.BlockSpec((tm, d), lambda i, f: (i, 0)),
        out_shape=jax.ShapeDtypeStruct((n, d), F32),
        scratch_shapes=[pltpu.VMEM((tm, tf), BF16)]
        + [pltpu.VMEM((part_rows, d), BF16)] * FFN_ROW_PARTS
        + [pltpu.VMEM((part_rows, 2 * FFN_SUB_COLS), F32)] * (FFN_ROW_PARTS * (tf // FFN_SUB_COLS)),
        compiler_params=_params(("arbitrary", "arbitrary")),
        name="ffn",
    )(*args)


def _rope_tables(seq_len):
    rows = seq_len // GRID_W
    row = jnp.repeat(jnp.arange(rows, dtype=F32), GRID_W)
    col = jnp.tile(jnp.arange(GRID_W, dtype=F32), rows)
    n_freq = HEAD_DIM // 4
    inv = ROPE_THETA ** (-jnp.arange(n_freq, dtype=F32) / n_freq)
    ang = jnp.concatenate([row[:, None] * inv, col[:, None] * inv], axis=-1)
    cos, sin = jnp.cos(ang), jnp.sin(ang)
    reps = V7X_LANES // HEAD_DIM
    return jnp.tile(jnp.concatenate([cos, cos], -1), (1, reps)), jnp.tile(jnp.concatenate([-sin, sin], -1), (1, reps))


def _gmajor_cols(w, d_in):
    return w.reshape(d_in, N_KV_HEADS, Q_PER_KV, HEAD_DIM).transpose(0, 2, 1, 3).reshape(d_in, -1)


def kernel(x_prompt, x_sample, cache_k, cache_v, c, c_ctx, w_ada, b_ada, g_norm1, w_in, attn_sink, conv_dw_w, conv_dw_b, conv_ln_g, conv_ln_b, gmlp_ln_g, gmlp_ln_b, gmlp_ws, gmlp_bs, w_out, g_norm2, w_up, ffn_dw_w, ffn_dw_b, w_down, g_final):
    b1, l1, d = x_prompt.shape
    b2, l2, _ = x_sample.shape
    depth = w_ada.shape[0]
    past = cache_k.shape[2]
    kv_dim = N_KV_HEADS * HEAD_DIM
    attn_dim = N_KV_HEADS * Q_PER_KV * HEAD_DIM
    conv_dim = conv_dw_w.shape[-1]
    gmlp_dim = gmlp_ln_g.shape[1] * gmlp_ln_g.shape[2]
    dims = (attn_dim, kv_dim, conv_dim, gmlp_dim)

    n_cond = 1 + b2
    cond = jnp.concatenate([c_ctx[None, :], c], axis=0)
    cond = jnp.pad(cond, ((0, (-n_cond) % V7X_SUBLANES), (0, 0)))
    mod_all = _ada(cond, w_ada, b_ada)

    rope = _rope_tables(l2)
    xp = x_prompt.reshape(b1 * l1, d)
    xs = x_sample.reshape(b2 * l2, d)
    new_k, new_v = [], []

    for l in range(depth):
        w_in_l = jnp.concatenate([_gmajor_cols(w_in[l, :, :attn_dim], d), w_in[l, :, attn_dim:]], axis=1).astype(BF16)
        w_out_l = jnp.concatenate([_gmajor_cols(w_out[l, :attn_dim].T, d).T, w_out[l, attn_dim:]], axis=0).astype(BF16)
        w_up_l = w_up[l].astype(BF16)
        w_down_l = w_down[l].astype(BF16)
        gmlp = (gmlp_ln_g[l].reshape(1, gmlp_dim), gmlp_ln_b[l].reshape(1, gmlp_dim), gmlp_ws[l].astype(BF16),
                jnp.repeat(gmlp_bs[l].T, GMLP_GROUP_DIM, axis=1))
        conv = (jnp.broadcast_to(conv_dw_w[l][:, None, :], (CONV_K, V7X_SUBLANES, conv_dim)), conv_dw_b[l][None, :],
                conv_ln_g[l][None, :], conv_ln_b[l][None, :])
        mod = mod_all[l].reshape(-1, 6, d)
        g_last = g_final[None, :] if l == depth - 1 else None

        def layer(x, seq_len, mod_rows, per_seq_mod, rope_l, window_kv, ctx_kv, emit_kv):
            def tiling(tm):
                tps = max(seq_len // tm, 1)
                return min(tm, seq_len), tps, tps if per_seq_mod else x.shape[0] // tm

            tm = MIX_TILE_ROWS
            seg, tps, tpm = tiling(tm)
            outs = _inproj(x, mod_rows, g_norm1[l][None, :], w_in_l, rope_l, gmlp, tm=tm, tiles_per_mod=tpm,
                           dims=dims, emit_kv_f32=emit_kv)
            q, k, v, glu, gm = outs[:5]
            if window_kv:
                attn = _attention(q, k, v, ctx_kv[0], ctx_kv[1], attn_sink[l], nb=seq_len // BLOCK, local=True)
            else:
                nseq = x.shape[0] // seq_len
                attn = _attention(q, None, None, k.reshape(nseq, seq_len, kv_dim), v.reshape(nseq, seq_len, kv_dim),
                                  attn_sink[l], nb=seq_len // BLOCK, local=False)
            x = _outproj(attn, glu, gm, x, mod_rows, conv, w_out_l, tm=tm, seg=seg, tiles_per_seq=tps,
                         tiles_per_mod=tpm)
            tmf = FFN_TILE_ROWS
            segf, tpsf, tpmf = tiling(tmf)
            x = _ffn(x, mod_rows, g_norm2[l][None, :], w_up_l, ffn_dw_w[l], ffn_dw_b[l][None, :], w_down_l, g_last,
                     tm=tmf, tf=FFN_TILE_COLS, seg=segf, tiles_per_seq=tpsf, tiles_per_mod=tpmf)
            return x, outs[5:]

        xp, kv_f32 = layer(xp, l1, mod[0:1], False, None, False, None, True)
        new_k.append(kv_f32[0].reshape(b1, l1, N_KV_HEADS, HEAD_DIM))
        new_v.append(kv_f32[1].reshape(b1, l1, N_KV_HEADS, HEAD_DIM))
        ctx_kv = (cache_k[:, l].reshape(b2, past, kv_dim).astype(BF16), cache_v[:, l].reshape(b2, past, kv_dim).astype(BF16))
        xs, _ = layer(xs, l2, mod[1:1 + b2], True, rope, True, ctx_kv, False)

    return (xp.reshape(b1, l1, d), xs.reshape(b2, l2, d), jnp.stack(new_k, axis=1), jnp.stack(new_v, axis=1))
```

```python
import functools

import jax
import jax.numpy as jnp
from jax import lax
from jax.experimental import pallas as pl
from jax.experimental.pallas import tpu as pltpu

F32 = jnp.float32
BF16 = jnp.bfloat16

HEAD_DIM = 64
N_KV_HEADS = 4
Q_PER_KV = 4
GRID_W = 64
BLOCK = 128
CHUNK = 128
GMLP_GROUP_DIM = 128
CONV_K = 31
FFN_K = 3
ROPE_THETA = 10000.0
EPS = 1e-6
NEG = -1e30
SCALE = HEAD_DIM ** -0.5

V7X_LANES = 128
V7X_SUBLANES = 8
V7X_VMEM_LIMIT_BYTES = 56 * 1024 * 1024

MIX_TILE_ROWS = 512
FFN_TILE_ROWS = 1024
FFN_TILE_COLS = 512
FFN_SUB_COLS = 256
FFN_ROW_PARTS = 2
FFN_GATE_CHUNKS = 4
OUT_COLS = 256

CONV_HALO = 16
CONV_ROWS = 64
FFN_HALO = V7X_SUBLANES
NORM_ROWS = 256


def _sigmoid(x):
    return 1.0 / (1.0 + jnp.exp(-x))


def _rms_mod(x, g, sc, sh):
    ms = jnp.mean(x * x, axis=-1, keepdims=True)
    return (x * lax.rsqrt(ms + EPS) * g) * (1.0 + sc) + sh


def _const_spec(shape):
    nd = len(shape)
    return pl.BlockSpec(shape, lambda *_: (0,) * nd, pipeline_mode=pl.Buffered(1))


def _params(sem):
    return pltpu.CompilerParams(dimension_semantics=sem, vmem_limit_bytes=V7X_VMEM_LIMIT_BYTES)


def _ada_kernel(c_ref, w_ref, b_ref, o_ref):
    c = c_ref[...]
    s = (c * _sigmoid(c)).astype(BF16)
    o_ref[0] = jnp.dot(s, w_ref[0].astype(BF16), preferred_element_type=F32) + b_ref[0]


def _ada(cond, w_ada, b_ada):
    depth, d, n = w_ada.shape
    r = cond.shape[0]
    tn = 1024
    return pl.pallas_call(
        _ada_kernel,
        grid=(depth, n // tn),
        in_specs=[
            pl.BlockSpec((r, d), lambda l, j: (0, 0)),
            pl.BlockSpec((1, d, tn), lambda l, j: (l, 0, j)),
            pl.BlockSpec((1, 1, tn), lambda l, j: (l, 0, j)),
        ],
        out_specs=pl.BlockSpec((1, r, tn), lambda l, j: (l, 0, j)),
        out_shape=jax.ShapeDtypeStruct((depth, r, n), F32),
        compiler_params=_params(("arbitrary", "arbitrary")),
        name="ada",
    )(cond, w_ada, b_ada.reshape(depth, 1, n))


def _rope(t, cos, sin):
    outs = []
    lane = lax.broadcasted_iota(jnp.int32, (t.shape[0], V7X_LANES), 1)
    first = (lane % HEAD_DIM) < (HEAD_DIM // 2)
    for j in range(t.shape[1] // V7X_LANES):
        tj = t[:, j * V7X_LANES:(j + 1) * V7X_LANES]
        rot = jnp.where(first, pltpu.roll(tj, V7X_LANES - HEAD_DIM // 2, 1), pltpu.roll(tj, HEAD_DIM // 2, 1))
        outs.append(tj * cos + rot * sin)
    return jnp.concatenate(outs, axis=1)


def _inproj_kernel(*refs, tm, dims, use_rope, emit_kv_f32):
    attn_dim, kv_dim, conv_dim, gmlp_dim = dims
    it = iter(refs)
    x_ref, mod_ref, g_ref, w_ref = next(it), next(it), next(it), next(it)
    if use_rope:
        cos_ref, sin_ref = next(it), next(it)
    lng_ref, lnb_ref, ws_ref, bs_ref = next(it), next(it), next(it), next(it)
    q_ref, k_ref, v_ref, glu_ref, gm_ref = next(it), next(it), next(it), next(it), next(it)
    if emit_kv_f32:
        kf_ref, vf_ref = next(it), next(it)

    h = _rms_mod(x_ref[...], g_ref[...], mod_ref[0, 1:2, :], mod_ref[0, 0:1, :]).astype(BF16)

    def proj(lo, n):
        return jnp.dot(h, w_ref[:, lo:lo + n], preferred_element_type=F32)

    o_k = attn_dim
    o_v = o_k + kv_dim
    o_a = o_v + kv_dim
    o_g = o_a + conv_dim
    o_u = o_g + conv_dim
    o_w = o_u + gmlp_dim

    q = proj(0, attn_dim)
    k = proj(o_k, kv_dim)
    v = proj(o_v, kv_dim)
    if emit_kv_f32:
        kf_ref[...] = k
        vf_ref[...] = v
    if use_rope:
        cos, sin = cos_ref[...], sin_ref[...]
        q = _rope(q, cos, sin)
        k = _rope(k, cos, sin)
    q_ref[...] = q.astype(BF16)
    k_ref[...] = k.astype(BF16)
    v_ref[...] = v.astype(BF16)

    glu_ref[...] = proj(o_a, conv_dim) * _sigmoid(proj(o_g, conv_dim))

    u = jax.nn.gelu(proj(o_u, gmlp_dim))
    w = jax.nn.gelu(proj(o_w, gmlp_dim))
    for g in range(gmlp_dim // GMLP_GROUP_DIM):
        cs = slice(g * GMLP_GROUP_DIM, (g + 1) * GMLP_GROUP_DIM)
        wg = w[:, cs]
        mu = jnp.mean(wg, axis=-1, keepdims=True)
        var = jnp.mean(jnp.square(wg - mu), axis=-1, keepdims=True)
        wn = ((wg - mu) * lax.rsqrt(var + EPS) * lng_ref[:, cs] + lnb_ref[:, cs]).astype(BF16)
        for c in range(tm // CHUNK):
            rs = slice(c * CHUNK, (c + 1) * CHUNK)
            sv = jnp.dot(ws_ref[g], wn[rs, :], preferred_element_type=F32) + bs_ref[:, cs]
            gm_ref[rs, cs] = (u[rs, cs] * sv).astype(BF16)


def _inproj(x, mod, g_norm, w_in, rope, gmlp, *, tm, tiles_per_mod, dims, emit_kv_f32):
    n, d = x.shape
    attn_dim, kv_dim, conv_dim, gmlp_dim = dims
    lng, lnb, ws, bs = gmlp
    row = lambda w: pl.BlockSpec((tm, w), lambda i: (i, 0))
    in_specs = [
        row(d),
        pl.BlockSpec((1, 6, d), lambda i: (i // tiles_per_mod, 0, 0)),
        _const_spec((1, d)),
        _const_spec(w_in.shape),
    ]
    args = [x, mod, g_norm, w_in]
    if rope is not None:
        tiles_per_seq = rope[0].shape[0] // tm
        in_specs += [pl.BlockSpec((tm, V7X_LANES), lambda i: (i % tiles_per_seq, 0))] * 2
        args += list(rope)
    in_specs += [_const_spec(lng.shape), _const_spec(lnb.shape), _const_spec(ws.shape), _const_spec(bs.shape)]
    args += [lng, lnb, ws, bs]
    out_specs = [row(attn_dim), row(kv_dim), row(kv_dim), row(conv_dim), row(gmlp_dim)]
    out_shape = [
        jax.ShapeDtypeStruct((n, attn_dim), BF16),
        jax.ShapeDtypeStruct((n, kv_dim), BF16),
        jax.ShapeDtypeStruct((n, kv_dim), BF16),
        jax.ShapeDtypeStruct((n, conv_dim), F32),
        jax.ShapeDtypeStruct((n, gmlp_dim), BF16),
    ]
    if emit_kv_f32:
        out_specs += [row(kv_dim), row(kv_dim)]
        out_shape += [jax.ShapeDtypeStruct((n, kv_dim), F32)] * 2
    return pl.pallas_call(
        functools.partial(_inproj_kernel, tm=tm, dims=dims, use_rope=rope is not None, emit_kv_f32=emit_kv_f32),
        grid=(n // tm,),
        in_specs=in_specs,
        out_specs=out_specs,
        out_shape=out_shape,
        compiler_params=_params(("arbitrary",)),
        name="inproj",
    )(*args)


def _attn_kernel(*refs, nb, local):
    it = iter(refs)
    sink_ref, q_ref = next(it), next(it)
    if local:
        kl_refs = (next(it), next(it), next(it))
        vl_refs = (next(it), next(it), next(it))
    ck_ref, cv_ref, o_ref = next(it), next(it), next(it)

    rows = Q_PER_KV * BLOCK
    lane = lax.broadcasted_iota(jnp.int32, (rows, V7X_LANES), 1)
    rowg = lax.broadcasted_iota(jnp.int32, (rows, 1), 0) // BLOCK
    upper = lane >= HEAD_DIM
    nt = (((1,), (1,)), ((), ()))

    if local:
        n = pl.program_id(0) % nb
        qi = lax.broadcasted_iota(jnp.int32, (rows, 3 * BLOCK), 0) % BLOCK
        kj = lax.broadcasted_iota(jnp.int32, (rows, 3 * BLOCK), 1)
        lo = jnp.where(n == 0, BLOCK, 0)
        hi = jnp.where(n == nb - 1, 2 * BLOCK, 3 * BLOCK)
        valid = (kj >= qi) & (kj <= qi + 2 * BLOCK) & (kj >= lo) & (kj < hi)

    for p in range(N_KV_HEADS // 2):
        ls = slice(p * V7X_LANES, (p + 1) * V7X_LANES)
        qs = jnp.concatenate(
            [q_ref[:, g * 2 * V7X_LANES + p * V7X_LANES: g * 2 * V7X_LANES + (p + 1) * V7X_LANES]
             for g in range(Q_PER_KV)], axis=0) * SCALE
        ck = ck_ref[0, :, ls]
        cv = cv_ref[0, :, ls]
        if local:
            kl = jnp.concatenate([r[:, ls] for r in kl_refs], axis=0)
            vl = jnp.concatenate([r[:, ls] for r in vl_refs], axis=0)
        outs = []
        for hh in range(2):
            kvh = 2 * p + hh
            qm = jnp.where(upper if hh else jnp.logical_not(upper), qs, jnp.zeros_like(qs))
            sk = jnp.zeros((rows, 1), F32)
            for g in range(Q_PER_KV):
                sk = jnp.where(rowg == g, sink_ref[kvh * Q_PER_KV + g], sk)
            s_ctx = lax.dot_general(qm, ck, nt, preferred_element_type=F32)
            m = jnp.maximum(jnp.max(s_ctx, axis=-1, keepdims=True), sk)
            if local:
                s_loc = lax.dot_general(qm, kl, nt, preferred_element_type=F32)
                s_loc = jnp.where(valid, s_loc, NEG)
                m = jnp.maximum(m, jnp.max(s_loc, axis=-1, keepdims=True))
            p_ctx = jnp.exp(s_ctx - m)
            den = jnp.sum(p_ctx, axis=-1, keepdims=True) + jnp.exp(sk - m)
            acc = jnp.dot(p_ctx.astype(BF16), cv, preferred_element_type=F32)
            if local:
                p_loc = jnp.exp(s_loc - m)
                den = den + jnp.sum(p_loc, axis=-1, keepdims=True)
                acc = acc + jnp.dot(p_loc.astype(BF16), vl, preferred_element_type=F32)
            outs.append(acc / den)
        o = jnp.where(upper, outs[1], outs[0]).astype(BF16)
        for g in range(Q_PER_KV):
            o_ref[:, g * 2 * V7X_LANES + p * V7X_LANES: g * 2 * V7X_LANES + (p + 1) * V7X_LANES] = (
                o[g * BLOCK:(g + 1) * BLOCK, :])


def _attention(q, k, v, ck, cv, sink, *, nb, local):
    n, attn_dim = q.shape
    kv_dim = ck.shape[-1]
    nblk = n // BLOCK
    in_specs = [pl.BlockSpec(memory_space=pltpu.SMEM), pl.BlockSpec((BLOCK, attn_dim), lambda i: (i, 0))]
    args = [sink, q]
    if local:
        maps = [lambda i: (jnp.maximum(i - 1, 0), 0), lambda i: (i, 0), lambda i: (jnp.minimum(i + 1, nblk - 1), 0)]
        in_specs += [pl.BlockSpec((BLOCK, kv_dim), m) for m in maps] * 2
        args += [k, k, k, v, v, v]
    in_specs += [pl.BlockSpec((1,) + ck.shape[1:], lambda i: (i // nb, 0, 0))] * 2
    args += [ck, cv]
    return pl.pallas_call(
        functools.partial(_attn_kernel, nb=nb, local=local),
        grid=(nblk,),
        in_specs=in_specs,
        out_specs=pl.BlockSpec((BLOCK, attn_dim), lambda i: (i, 0)),
        out_shape=jax.ShapeDtypeStruct((n, attn_dim), BF16),
        compiler_params=_params(("arbitrary",)),
        name="attention",
    )(*args)


def _outproj_kernel(attn_ref, glu_ref, glup_ref, glun_ref, gm_ref, x_ref, mod_ref, cw_ref, cb_ref,
                    clg_ref, clb_ref, wo_ref, o_ref, pad_ref, shift_ref, conv_ref, *, tm, seg, tiles_per_seq):
    n = pl.program_id(0) % tiles_per_seq
    attn_dim = attn_ref.shape[1]
    conv_dim = glu_ref.shape[1]
    stride = seg + 2 * CONV_HALO
    total = (tm // seg) * stride
    zeros = jnp.zeros((CONV_HALO, conv_dim), F32)

    for s in range(tm // seg):
        base = s * stride
        lead, trail = zeros, zeros
        if tiles_per_seq > 1:
            lead = jnp.where(n > 0, glup_ref[...], 0.0)
            trail = jnp.where(n < tiles_per_seq - 1, glun_ref[...], 0.0)
        pad_ref[base:base + CONV_HALO, :] = lead
        pad_ref[base + CONV_HALO:base + CONV_HALO + seg, :] = glu_ref[s * seg:(s + 1) * seg, :]
        pad_ref[base + CONV_HALO + seg:base + stride, :] = trail
    for b in range(1, V7X_SUBLANES):
        shift_ref[b - 1, 0:total - V7X_SUBLANES, :] = pad_ref[b:b + total - V7X_SUBLANES, :]

    first_tap = CONV_HALO - (CONV_K - 1) // 2
    groups = CONV_ROWS // V7X_SUBLANES

    def conv_block(s, r0):
        accs = [jnp.zeros((V7X_SUBLANES, conv_dim), F32) + cb_ref[...]] * groups
        for j in range(CONV_K):
            off = first_tap + j
            lo = s * stride + r0 + (off // V7X_SUBLANES) * V7X_SUBLANES
            sub = off % V7X_SUBLANES
            w = cw_ref[j]
            for q in range(groups):
                rs = slice(lo + q * V7X_SUBLANES, lo + (q + 1) * V7X_SUBLANES)
                win = pad_ref[rs, :] if sub == 0 else shift_ref[sub - 1, rs, :]
                accs[q] = accs[q] + win * w
        acc = jnp.concatenate(accs, axis=0)
        mu = jnp.mean(acc, axis=-1, keepdims=True)
        var = jnp.mean(jnp.square(acc - mu), axis=-1, keepdims=True)
        y = (acc - mu) * lax.rsqrt(var + EPS) * clg_ref[...] + clb_ref[...]
        conv_ref[s * seg + r0:s * seg + r0 + CONV_ROWS, :] = (y * _sigmoid(y)).astype(BF16)

    blocks = [(s, r0) for s in range(tm // seg) for r0 in range(0, seg, CONV_ROWS)]
    n_col_tiles = o_ref.shape[1] // OUT_COLS
    per_tile = -(-len(blocks) // n_col_tiles)
    for k in range(n_col_tiles):
        cols = slice(k * OUT_COLS, (k + 1) * OUT_COLS)
        part = jnp.dot(attn_ref[...], wo_ref[0:attn_dim, cols], preferred_element_type=F32)
        o_ref[:, cols] = part + jnp.dot(gm_ref[...], wo_ref[attn_dim + conv_dim:, cols], preferred_element_type=F32)
        for blk in blocks[k * per_tile:(k + 1) * per_tile]:
            conv_block(*blk)
    g1 = mod_ref[0, 2:3, :]
    for k in range(n_col_tiles):
        cols = slice(k * OUT_COLS, (k + 1) * OUT_COLS)
        part = jnp.dot(conv_ref[...], wo_ref[attn_dim:attn_dim + conv_dim, cols], preferred_element_type=F32)
        o_ref[:, cols] = x_ref[:, cols] + g1[:, cols] * (o_ref[:, cols] + part)


def _outproj(attn, glu, gm, x, mod, conv, w_out, *, tm, seg, tiles_per_seq, tiles_per_mod):
    n, d = x.shape
    cw, cb, clg, clb = conv
    conv_dim = glu.shape[1]
    hb = tm // CONV_HALO
    nhb = n // CONV_HALO
    pad_rows = (tm // seg) * (seg + 2 * CONV_HALO)
    row = lambda w: pl.BlockSpec((tm, w), lambda i: (i, 0))
    return pl.pallas_call(
        functools.partial(_outproj_kernel, tm=tm, seg=seg, tiles_per_seq=tiles_per_seq),
        grid=(n // tm,),
        in_specs=[
            row(attn.shape[1]),
            row(conv_dim),
            pl.BlockSpec((CONV_HALO, conv_dim), lambda i: (jnp.maximum(i * hb - 1, 0), 0)),
            pl.BlockSpec((CONV_HALO, conv_dim), lambda i: (jnp.minimum((i + 1) * hb, nhb - 1), 0)),
            row(gm.shape[1]),
            row(d),
            pl.BlockSpec((1, 6, d), lambda i: (i // tiles_per_mod, 0, 0)),
            _const_spec(cw.shape), _const_spec(cb.shape), _const_spec(clg.shape), _const_spec(clb.shape),
            _const_spec(w_out.shape),
        ],
        out_specs=row(d),
        out_shape=jax.ShapeDtypeStruct((n, d), F32),
        scratch_shapes=[
            pltpu.VMEM((pad_rows, conv_dim), F32),
            pltpu.VMEM((V7X_SUBLANES - 1, pad_rows, conv_dim), F32),
            pltpu.VMEM((tm, conv_dim), BF16),
        ],
        compiler_params=_params(("arbitrary",)),
        name="outproj",
    )(attn, glu, glu, glu, gm, x, mod, cw, cb, clg, clb, w_out)


def _ffn_kernel(*refs, tm, seg, tiles_per_seq, nf, final):
    it = iter(refs)
    zero_ref, x_ref, xp_ref, xn_ref, mod_ref, g_ref = (next(it) for _ in range(6))
    wa_ref, wb_ref, cwa_ref, cwb_ref, cba_ref, cbb_ref, wd_ref = (next(it) for _ in range(7))
    if final:
        gf_ref = next(it)
    o_ref, act_ref = next(it), next(it)
    tf = wa_ref.shape[1]
    sub = FFN_SUB_COLS
    n_sub = tf // sub
    parts = FFN_ROW_PARTS
    h_refs = [next(it) for _ in range(parts)]
    u_refs = [[next(it) for _ in range(parts)] for _ in range(n_sub)]

    f = pl.program_id(1)
    n = pl.program_id(0) % tiles_per_seq
    d = x_ref.shape[1]
    pr = tm // parts
    prev_halo, next_halo = pr, pr + FFN_HALO
    row0 = zero_ref[0]

    @pl.when(f == 0)
    def _():
        g, sc, sh = g_ref[...], mod_ref[0, 4:5, :], mod_ref[0, 3:4, :]
        gs = g * (1.0 + sc)

        def norm(x):
            ms = jnp.mean(x * x, axis=-1, keepdims=True)
            return x * lax.rsqrt(ms + EPS) * gs + sh

        nr = min(NORM_ROWS, pr)
        for p in range(parts):
            def norm_rows(r, carry, p=p):
                lo = pl.multiple_of(r * nr, nr)
                h_refs[p][pl.ds(lo, nr), :] = norm(x_ref[pl.ds(p * pr + lo, nr), :]).astype(BF16)
                return carry

            lax.fori_loop(0, pr // nr, norm_rows, 0)
            lo, hi = p * pr, (p + 1) * pr
            if p == 0:
                before = jnp.where(n > 0, norm(xp_ref[...]), 0.0)
            elif lo % seg == 0:
                before = jnp.zeros((FFN_HALO, d), F32)
            else:
                before = norm(x_ref[lo - FFN_HALO:lo, :])
            if p == parts - 1:
                after = jnp.where(n < tiles_per_seq - 1, norm(xn_ref[...]), 0.0)
            elif hi % seg == 0:
                after = jnp.zeros((FFN_HALO, d), F32)
            else:
                after = norm(x_ref[hi:hi + FFN_HALO, :])
            h_refs[p][prev_halo:prev_halo + FFN_HALO, :] = before.astype(BF16)
            h_refs[p][next_halo:next_halo + FFN_HALO, :] = after.astype(BF16)
        o_ref[...] = jnp.zeros((tm, d), F32)

    def up(s, p):
        h = h_refs[p][...]
        for j, w_ref in enumerate((wa_ref, wb_ref)):
            u_refs[s][p][:, j * sub:(j + 1) * sub] = jnp.dot(h, w_ref[:, s * sub:(s + 1) * sub],
                                                             preferred_element_type=F32)

    def gate_rows(s, p, r0, r1):
        rows = r1 - r0
        base = p * pr
        zeros = jnp.zeros((FFN_HALO, sub), F32)
        mid = slice(FFN_HALO, FFN_HALO + rows)
        row = lax.broadcasted_iota(jnp.int32, (rows, 1), 0) + (base + r0)
        inner = [b for b in range(seg, tm, seg) if base + r0 < b < base + r1]
        hid = slice(s * sub, (s + 1) * sub)

        def u_rows(start, nrows, cols):
            return u_refs[s][p][pl.ds(pl.multiple_of(row0 + start, V7X_SUBLANES), nrows), cols]

        def conv(j, cw_ref, cb_ref):
            cols = slice(j * sub, (j + 1) * sub)
            if r0 == 0:
                top = u_rows(prev_halo, FFN_HALO, cols)
            elif (base + r0) % seg == 0:
                top = zeros
            else:
                top = u_rows(r0 - FFN_HALO, FFN_HALO, cols)
            if r1 == pr:
                bot = u_rows(next_halo, FFN_HALO, cols)
            elif (base + r1) % seg == 0:
                bot = zeros
            else:
                bot = u_rows(r1, FFN_HALO, cols)
            ext = jnp.concatenate([top, u_rows(r0, rows, cols), bot], axis=0)
            prev = pltpu.roll(ext, 1, 0)[mid]
            nxt = pltpu.roll(ext, rows + 2 * FFN_HALO - 1, 0)[mid]
            for b in inner:
                prev = jnp.where(row == b, 0.0, prev)
                nxt = jnp.where(row == b - 1, 0.0, nxt)
            return cw_ref[0:1, hid] * prev + cw_ref[1:2, hid] * ext[mid] + cw_ref[2:3, hid] * nxt + cb_ref[:, hid]

        a = conv(0, cwa_ref, cba_ref)
        b = conv(1, cwb_ref, cbb_ref)
        act_ref[base + r0:base + r1, hid] = (a * _sigmoid(a) * b).astype(BF16)

    chunk = pr // FFN_GATE_CHUNKS
    for s in range(n_sub):
        for p in range(parts):
            up(s, p)
    for s in range(n_sub):
        for p in range(parts):
            for k in range(FFN_GATE_CHUNKS):
                gate_rows(s, p, k * chunk, (k + 1) * chunk)
            rs = slice(p * pr, (p + 1) * pr)
            o_ref[rs, :] += jnp.dot(act_ref[rs, s * sub:(s + 1) * sub], wd_ref[s * sub:(s + 1) * sub, :],
                                    preferred_element_type=F32)

    @pl.when(f == nf - 1)
    def _():
        y = x_ref[...] + mod_ref[0, 5:6, :] * o_ref[...]
        if final:
            ms = jnp.mean(y * y, axis=-1, keepdims=True)
            y = y * lax.rsqrt(ms + EPS) * gf_ref[...]
        o_ref[...] = y


def _ffn(x, mod, g_norm, w_up, cw, cb, w_down, g_final, *, tm, tf, seg, tiles_per_seq, tiles_per_mod):
    n, d = x.shape
    d_ff = w_down.shape[0]
    nf = d_ff // tf
    hb = tm // FFN_HALO
    nhb = n // FFN_HALO
    final = g_final is not None
    part_rows = tm // FFN_ROW_PARTS + 2 * FFN_HALO
    in_specs = [
        pl.BlockSpec(memory_space=pltpu.SMEM),
        pl.BlockSpec((tm, d), lambda i, f: (i, 0), pipeline_mode=pl.Buffered(1)),
        pl.BlockSpec((FFN_HALO, d), lambda i, f: (jnp.maximum(i * hb - 1, 0), 0)),
        pl.BlockSpec((FFN_HALO, d), lambda i, f: (jnp.minimum((i + 1) * hb, nhb - 1), 0)),
        pl.BlockSpec((1, 6, d), lambda i, f: (i // tiles_per_mod, 0, 0)),
        _const_spec((1, d)),
        pl.BlockSpec((d, tf), lambda i, f: (0, f)),
        pl.BlockSpec((d, tf), lambda i, f: (0, nf + f)),
        pl.BlockSpec((FFN_K, tf), lambda i, f: (0, f)),
        pl.BlockSpec((FFN_K, tf), lambda i, f: (0, nf + f)),
        pl.BlockSpec((1, tf), lambda i, f: (0, f)),
        pl.BlockSpec((1, tf), lambda i, f: (0, nf + f)),
        pl.BlockSpec((tf, d), lambda i, f: (f, 0)),
    ]
    args = [jnp.zeros((1,), jnp.int32), x, x, x, mod, g_norm, w_up, w_up, cw, cw, cb, cb, w_down]
    if final:
        in_specs.append(_const_spec((1, d)))
        args.append(g_final)
    return pl.pallas_call(
        functools.partial(_ffn_kernel, tm=tm, seg=seg, tiles_per_seq=tiles_per_seq, nf=nf, final=final),
        grid=(n // tm, nf),
        in_specs=in_specs,
        out_specs=pl.BlockSpec((tm, d), lambda i, f: (i, 0)),
        out_shape=jax.ShapeDtypeStruct((n, d), F32),
        scratch_shapes=[pltpu.VMEM((tm, tf), BF16)]
        + [pltpu.VMEM((part_rows, d), BF16)] * FFN_ROW_PARTS
        + [pltpu.VMEM((part_rows, 2 * FFN_SUB_COLS), F32)] * (FFN_ROW_PARTS * (tf // FFN_SUB_COLS)),
        compiler_params=_params(("arbitrary", "arbitrary")),
        name="ffn",
    )(*args)


def _rope_tables(seq_len):
    rows = seq_len // GRID_W
    row = jnp.repeat(jnp.arange(rows, dtype=F32), GRID_W)
    col = jnp.tile(jnp.arange(GRID_W, dtype=F32), rows)
    n_freq = HEAD_DIM // 4
    inv = ROPE_THETA ** (-jnp.arange(n_freq, dtype=F32) / n_freq)
    ang = jnp.concatenate([row[:, None] * inv, col[:, None] * inv], axis=-1)
    cos, sin = jnp.cos(ang), jnp.sin(ang)
    reps = V7X_LANES // HEAD_DIM
    return jnp.tile(jnp.concatenate([cos, cos], -1), (1, reps)), jnp.tile(jnp.concatenate([-sin, sin], -1), (1, reps))


def _gmajor_cols(w, d_in):
    return w.reshape(d_in, N_KV_HEADS, Q_PER_KV, HEAD_DIM).transpose(0, 2, 1, 3).reshape(d_in, -1)


def kernel(x_prompt, x_sample, cache_k, cache_v, c, c_ctx, w_ada, b_ada, g_norm1, w_in, attn_sink, conv_dw_w, conv_dw_b, conv_ln_g, conv_ln_b, gmlp_ln_g, gmlp_ln_b, gmlp_ws, gmlp_bs, w_out, g_norm2, w_up, ffn_dw_w, ffn_dw_b, w_down, g_final):
    b1, l1, d = x_prompt.shape
    b2, l2, _ = x_sample.shape
    depth = w_ada.shape[0]
    past = cache_k.shape[2]
    kv_dim = N_KV_HEADS * HEAD_DIM
    attn_dim = N_KV_HEADS * Q_PER_KV * HEAD_DIM
    conv_dim = conv_dw_w.shape[-1]
    gmlp_dim = gmlp_ln_g.shape[1] * gmlp_ln_g.shape[2]
    dims = (attn_dim, kv_dim, conv_dim, gmlp_dim)

    n_cond = 1 + b2
    cond = jnp.concatenate([c_ctx[None, :], c], axis=0)
    cond = jnp.pad(cond, ((0, (-n_cond) % V7X_SUBLANES), (0, 0)))
    mod_all = _ada(cond, w_ada, b_ada)

    rope = _rope_tables(l2)
    xp = x_prompt.reshape(b1 * l1, d)
    xs = x_sample.reshape(b2 * l2, d)
    new_k, new_v = [], []

    for l in range(depth):
        w_in_l = jnp.concatenate([_gmajor_cols(w_in[l, :, :attn_dim], d), w_in[l, :, attn_dim:]], axis=1).astype(BF16)
        w_out_l = jnp.concatenate([_gmajor_cols(w_out[l, :attn_dim].T, d).T, w_out[l, attn_dim:]], axis=0).astype(BF16)
        w_up_l = w_up[l].astype(BF16)
        w_down_l = w_down[l].astype(BF16)
        gmlp = (gmlp_ln_g[l].reshape(1, gmlp_dim), gmlp_ln_b[l].reshape(1, gmlp_dim), gmlp_ws[l].astype(BF16),
                jnp.repeat(gmlp_bs[l].T, GMLP_GROUP_DIM, axis=1))
        conv = (jnp.broadcast_to(conv_dw_w[l][:, None, :], (CONV_K, V7X_SUBLANES, conv_dim)), conv_dw_b[l][None, :],
                conv_ln_g[l][None, :], conv_ln_b[l][None, :])
        mod = mod_all[l].reshape(-1, 6, d)
        g_last = g_final[None, :] if l == depth - 1 else None

        def layer(x, seq_len, mod_rows, per_seq_mod, rope_l, window_kv, ctx_kv, emit_kv):
            def tiling(tm):
                tps = max(seq_len // tm, 1)
                return min(tm, seq_len), tps, tps if per_seq_mod else x.shape[0] // tm

            tm = MIX_TILE_ROWS
            seg, tps, tpm = tiling(tm)
            outs = _inproj(x, mod_rows, g_norm1[l][None, :], w_in_l, rope_l, gmlp, tm=tm, tiles_per_mod=tpm,
                           dims=dims, emit_kv_f32=emit_kv)
            q, k, v, glu, gm = outs[:5]
            if window_kv:
                attn = _attention(q, k, v, ctx_kv[0], ctx_kv[1], attn_sink[l], nb=seq_len // BLOCK, local=True)
            else:
                nseq = x.shape[0] // seq_len
                attn = _attention(q, None, None, k.reshape(nseq, seq_len, kv_dim), v.reshape(nseq, seq_len, kv_dim),
                                  attn_sink[l], nb=seq_len // BLOCK, local=False)
            x = _outproj(attn, glu, gm, x, mod_rows, conv, w_out_l, tm=tm, seg=seg, tiles_per_seq=tps,
                         tiles_per_mod=tpm)
            tmf = FFN_TILE_ROWS
            segf, tpsf, tpmf = tiling(tmf)
            x = _ffn(x, mod_rows, g_norm2[l][None, :], w_up_l, ffn_dw_w[l], ffn_dw_b[l][None, :], w_down_l, g_last,
                     tm=tmf, tf=FFN_TILE_COLS, seg=segf, tiles_per_seq=tpsf, tiles_per_mod=tpmf)
            return x, outs[5:]

        xp, kv_f32 = layer(xp, l1, mod[0:1], False, None, False, None, True)
        new_k.append(kv_f32[0].reshape(b1, l1, N_KV_HEADS, HEAD_DIM))
        new_v.append(kv_f32[1].reshape(b1, l1, N_KV_HEADS, HEAD_DIM))
        ctx_kv = (cache_k[:, l].reshape(b2, past, kv_dim).astype(BF16), cache_v[:, l].reshape(b2, past, kv_dim).astype(BF16))
        xs, _ = layer(xs, l2, mod[1:1 + b2], True, rope, True, ctx_kv, False)

    return (xp.reshape(b1, l1, d), xs.reshape(b2, l2, d), jnp.stack(new_k, axis=1), jnp.stack(new_v, axis=1))
```
